```python
import math
import jax, jax.numpy as jnp
from jax import lax
import numpy as np

D_MODEL = 2048
BATCH = 4
SEQ = 8192
DEPTH = 1
DEC_BATCH = 8
DEC_SEQ = 16
PAST_LEN = 4096

CHUNK = 64
GDN_HEADS = 8
GDN_DK = 128
GDN_DV = 128
CONV_W = 4
RET_HEADS = 4
RET_DK = 256
RET_DV = 256
GDN_QK = GDN_HEADS * GDN_DK
GDN_V = GDN_HEADS * GDN_DV
GDN_CONV_DIM = 2 * GDN_QK + GDN_V
RET_QK = RET_HEADS * RET_DK
RET_V = RET_HEADS * RET_DV
MIX_WIDTH = GDN_V + RET_V
PROJ_WIDTH = GDN_CONV_DIM + GDN_V + 2 * GDN_HEADS + 2 * RET_QK + 2 * RET_V
D_FF = -(-8 * D_MODEL // (3 * 256)) * 256
RMS_EPS = 1e-6
GN_EPS = 1e-5
L2_EPS = 1e-6
ROPE_BASE = 10000.0

kernel_name = 'hybrid_gdn_retention_stream_step'


def _rmsnorm(x, w):
    xf = x.astype(jnp.float32)
    y = xf * lax.rsqrt(jnp.mean(xf * xf, axis=-1, keepdims=True) + RMS_EPS)
    return (y * w.astype(jnp.float32)).astype(x.dtype)


def _l2norm(x):
    xf = x.astype(jnp.float32)
    return xf * lax.rsqrt(jnp.sum(xf * xf, axis=-1, keepdims=True) + L2_EPS)


def _causal_conv(x, buf, w):
    L = x.shape[1]
    xp = jnp.concatenate([buf.astype(x.dtype), x], axis=1)
    y = sum(xp[:, j:j + L] * w[j] for j in range(CONV_W))
    return jax.nn.silu(y), xp[:, -(CONV_W - 1):]


def _rotary(x, pos):
    d = x.shape[-1]
    inv = 1.0 / (ROPE_BASE ** jnp.linspace(0.0, 1.0, d // 2, dtype=jnp.float32))
    ang = pos.astype(jnp.float32)[:, None] * inv[None, :]
    cos = jnp.cos(ang)[None, :, None, :]
    sin = jnp.sin(ang)[None, :, None, :]
    xf = x.astype(jnp.float32).reshape(*x.shape[:-1], d // 2, 2)
    x0, x1 = xf[..., 0], xf[..., 1]
    return jnp.stack([x0 * cos - x1 * sin, x1 * cos + x0 * sin], axis=-1).reshape(x.shape)


def _chunked_linear_attention(q, k, v, g, s0, beta=None):
    B, H, L, dk = q.shape
    dv = v.shape[-1]
    c = min(CHUNK, L)
    n = L // c

    def chunks(t):
        return jnp.moveaxis(t.reshape(B, H, n, c, *t.shape[3:]), 2, 0)

    q, k, v, g = chunks(q), chunks(k), chunks(v), chunks(g)
    gc = jnp.cumsum(g, axis=-1)
    causal = jnp.tril(jnp.ones((c, c), dtype=bool))
    diff = gc[..., :, None] - gc[..., None, :]
    dmat = jnp.where(causal, jnp.exp(jnp.where(causal, diff, 0.0)), 0.0)
    attn = jnp.einsum('nbhid,nbhjd->nbhij', q, k) * dmat
    delta = beta is not None
    if delta:
        beta = chunks(beta)
        kb = k * beta[..., None]
        strict = jnp.tril(jnp.ones((c, c), dtype=bool), -1)
        a = jnp.where(strict, jnp.einsum('nbhid,nbhjd->nbhij', kb, k) * dmat, 0.0)
        eye = jnp.eye(c, dtype=jnp.float32)
        t_inv = lax.linalg.triangular_solve(a + eye, jnp.broadcast_to(eye, a.shape),
                                            left_side=True, lower=True, unit_diagonal=True)
        u0 = jnp.einsum('nbhij,nbhje->nbhie', t_inv, v * beta[..., None])
        kcd = jnp.einsum('nbhij,nbhjd->nbhid', t_inv, kb * jnp.exp(gc)[..., None])
    else:
        u0 = v
    q_dec = q * jnp.exp(gc)[..., None]
    k_dec = k * jnp.exp(gc[..., -1:] - gc)[..., None]
    g_last = jnp.exp(gc[..., -1])
    xs = (q_dec, k_dec, attn, u0, g_last) + ((kcd,) if delta else ())

    def step(s, xs_i):
        qd, kd, at, u, gl = xs_i[:5]
        if delta:
            u = u - jnp.einsum('bhcd,bhde->bhce', xs_i[5], s)
        o = jnp.einsum('bhcd,bhde->bhce', qd, s) + jnp.einsum('bhij,bhje->bhie', at, u)
        s = s * gl[..., None, None] + jnp.einsum('bhcd,bhce->bhde', kd, u)
        return s, o

    s_end, o = lax.scan(step, s0, xs)
    o = jnp.moveaxis(o, 0, 2).reshape(B, H, L, dv)
    return o, s_end


def _layer(x, pos, conv_buf, s_gdn, s_ret, attn_norm_w, w_in, conv_w, a_log, dt_bias,
           gdn_norm_w, ret_gn_w, w_out, ffn_norm_w, w_gate_up, w_down):
    f32 = jnp.float32
    B, L, _ = x.shape
    h = _rmsnorm(x, attn_norm_w)
    p = h @ w_in
    cuts = np.cumsum([GDN_CONV_DIM, GDN_V, GDN_HEADS, GDN_HEADS, RET_QK, RET_QK, RET_V]).tolist()
    qkv_a, z_a, b_a, a_a, q_r, k_r, v_r, g_r = jnp.split(p, cuts, axis=-1)

    def heads(t, nh):
        return jnp.swapaxes(t.reshape(B, L, nh, -1), 1, 2)

    qkv_a, new_buf = _causal_conv(qkv_a, conv_buf, conv_w)
    q_a, k_a, v_a = jnp.split(qkv_a, [GDN_QK, 2 * GDN_QK], axis=-1)
    q_a = _l2norm(heads(q_a, GDN_HEADS)) * (GDN_DK ** -0.5)
    k_a = _l2norm(heads(k_a, GDN_HEADS))
    v_a = heads(v_a, GDN_HEADS).astype(f32)
    beta = jnp.swapaxes(jax.nn.sigmoid(b_a.astype(f32)), 1, 2)
    g_a = -jnp.exp(a_log.astype(f32)) * jax.nn.softplus(a_a.astype(f32) + dt_bias.astype(f32))
    g_a = jnp.swapaxes(g_a, 1, 2)
    o_a, s_gdn_new = _chunked_linear_attention(q_a, k_a, v_a, g_a, s_gdn.astype(f32), beta)
    o_a = jnp.swapaxes(o_a, 1, 2)
    o_a = o_a * lax.rsqrt(jnp.mean(o_a * o_a, axis=-1, keepdims=True) + RMS_EPS) * gdn_norm_w.astype(f32)
    o_a = o_a * jax.nn.silu(z_a.astype(f32)).reshape(B, L, GDN_HEADS, GDN_DV)

    q_r = _rotary(q_r.reshape(B, L, RET_HEADS, RET_DK), pos)
    k_r = _rotary(k_r.reshape(B, L, RET_HEADS, RET_DK), pos) * (RET_DK ** -0.5)
    v_r = v_r.reshape(B, L, RET_HEADS, RET_DV).astype(f32)
    log_gamma = jnp.log(1.0 - 2.0 ** (-5.0 - jnp.arange(RET_HEADS, dtype=f32)))
    g_ret = jnp.broadcast_to(log_gamma[None, :, None], (B, RET_HEADS, L))
    o_r, s_ret_new = _chunked_linear_attention(jnp.swapaxes(q_r, 1, 2), jnp.swapaxes(k_r, 1, 2),
                                               jnp.swapaxes(v_r, 1, 2), g_ret, s_ret.astype(f32))
    o_r = jnp.swapaxes(o_r, 1, 2)
    mu = jnp.mean(o_r, axis=-1, keepdims=True)
    var = jnp.mean(jnp.square(o_r - mu), axis=-1, keepdims=True)
    o_r = (o_r - mu) * lax.rsqrt(var + GN_EPS) * ret_gn_w.astype(f32).reshape(RET_HEADS, RET_DV)
    o_r = o_r * jax.nn.silu(g_r.astype(f32)).reshape(B, L, RET_HEADS, RET_DV)

    mix = jnp.concatenate([o_a.reshape(B, L, GDN_V), o_r.reshape(B, L, RET_V)], axis=-1).astype(x.dtype)
    x = x + mix @ w_out

    h = _rmsnorm(x, ffn_norm_w)
    gate, up = jnp.split(h @ w_gate_up, 2, axis=-1)
    x = x + (jax.nn.silu(gate) * up) @ w_down
    return x, new_buf, s_gdn_new.astype(s_gdn.dtype), s_ret_new.astype(s_ret.dtype)


def setup_inputs(seed: int = 0) -> dict:
    key = jax.random.key(seed)
    ks = jax.random.split(key, 17)
    nrm = jax.random.normal
    dt = jnp.exp(jax.random.uniform(ks[9], (DEPTH, GDN_HEADS), minval=math.log(1e-3), maxval=math.log(1e-1)))
    return {
        'x_prompt': nrm(ks[0], (BATCH, SEQ, D_MODEL), jnp.float32),
        'x_sample': nrm(ks[1], (DEC_BATCH, DEC_SEQ, D_MODEL), jnp.float32),
        'state_gdn_conv': nrm(ks[2], (DEPTH, DEC_BATCH, CONV_W - 1, GDN_CONV_DIM), jnp.float32),
        'state_gdn': 0.05 * nrm(ks[3], (DEPTH, DEC_BATCH, GDN_HEADS, GDN_DK, GDN_DV), jnp.float32),
        'state_ret': 0.1 * nrm(ks[4], (DEPTH, DEC_BATCH, RET_HEADS, RET_DK, RET_DV), jnp.float32),
        'attn_norm_w': 1.0 + 0.01 * nrm(ks[5], (DEPTH, D_MODEL), jnp.float32),
        'w_in': nrm(ks[6], (DEPTH, D_MODEL, PROJ_WIDTH), jnp.float32) * D_MODEL ** -0.5,
        'conv_w': nrm(ks[7], (DEPTH, CONV_W, GDN_CONV_DIM), jnp.float32) * CONV_W ** -0.5,
        'a_log': jnp.log(jax.random.uniform(ks[8], (DEPTH, GDN_HEADS), minval=1.0, maxval=16.0)),
        'dt_bias': dt + jnp.log(-jnp.expm1(-dt)),
        'gdn_norm_w': 1.0 + 0.01 * nrm(ks[10], (DEPTH, GDN_DV), jnp.float32),
        'ret_gn_w': 1.0 + 0.01 * nrm(ks[11], (DEPTH, RET_V), jnp.float32),
        'w_out': nrm(ks[12], (DEPTH, MIX_WIDTH, D_MODEL), jnp.float32) * MIX_WIDTH ** -0.5,
        'ffn_norm_w': 1.0 + 0.01 * nrm(ks[13], (DEPTH, D_MODEL), jnp.float32),
        'w_gate_up': nrm(ks[14], (DEPTH, D_MODEL, 2 * D_FF), jnp.float32) * D_MODEL ** -0.5,
        'w_down': nrm(ks[15], (DEPTH, D_FF, D_MODEL), jnp.float32) * D_FF ** -0.5,
        'final_norm_w': 1.0 + 0.01 * nrm(ks[16], (D_MODEL,), jnp.float32),
    }


def reference(x_prompt, x_sample, state_gdn_conv, state_gdn, state_ret, attn_norm_w, w_in, conv_w,
              a_log, dt_bias, gdn_norm_w, ret_gn_w, w_out, ffn_norm_w, w_gate_up, w_down, final_norm_w):
    bp = x_prompt.shape[0]
    pos_p = jnp.arange(x_prompt.shape[1])
    pos_s = PAST_LEN + jnp.arange(x_sample.shape[1])
    hp, hs = x_prompt, x_sample
    pc, pg, pr, sc, sg, sr = [], [], [], [], [], []
    for l in range(DEPTH):
        lw = (attn_norm_w[l], w_in[l], conv_w[l], a_log[l], dt_bias[l], gdn_norm_w[l], ret_gn_w[l],
              w_out[l], ffn_norm_w[l], w_gate_up[l], w_down[l])
        conv0 = jnp.zeros((bp, CONV_W - 1, GDN_CONV_DIM), x_prompt.dtype)
        sg0 = jnp.zeros((bp, GDN_HEADS, GDN_DK, GDN_DV), state_gdn.dtype)
        sr0 = jnp.zeros((bp, RET_HEADS, RET_DK, RET_DV), state_ret.dtype)
        hp, c_p, g_p, r_p = _layer(hp, pos_p, conv0, sg0, sr0, *lw)
        hs, c_s, g_s, r_s = _layer(hs, pos_s, state_gdn_conv[l], state_gdn[l], state_ret[l], *lw)
        pc.append(c_p); pg.append(g_p); pr.append(r_p)
        sc.append(c_s); sg.append(g_s); sr.append(r_s)
    y_prompt = _rmsnorm(hp, final_norm_w)
    y_sample = _rmsnorm(hs, final_norm_w)
    return (y_prompt, y_sample, jnp.stack(pc), jnp.stack(pg), jnp.stack(pr),
            jnp.stack(sc), jnp.stack(sg), jnp.stack(sr))
```

```python
import functools
import math

import jax
import jax.numpy as jnp
import numpy as np
from jax import lax
from jax.experimental import pallas as pl
from jax.experimental.pallas import tpu as pltpu

F32 = jnp.float32
BF16 = jnp.bfloat16

GDN_HEADS = 8
GDN_DK = 128
GDN_DV = 128
CONV_W = 4
RET_HEADS = 4
RET_DK = 256
RET_DV = 256
GDN_QK = GDN_HEADS * GDN_DK
GDN_V = GDN_HEADS * GDN_DV
GDN_CONV_DIM = 2 * GDN_QK + GDN_V
RET_QK = RET_HEADS * RET_DK
RET_V = RET_HEADS * RET_DV
CHUNK = 64
PAST_LEN = 4096
RMS_EPS = 1e-6
GN_EPS = 1e-5
L2_EPS = 1e-6
ROPE_BASE = 10000.0

LANES = 128
SUBLANES = 8
V7X_VMEM_BYTES = 64 * 1024 * 1024
VMEM_LIMIT_BYTES = 52 * 1024 * 1024
TRI_BLOCK = 16

P_WIDTH = 8 * 1024
BA_WIDTH = 2 * LANES


def _bdot(a, b):
    return jnp.dot(a.astype(BF16), b.astype(BF16), preferred_element_type=F32)


def _bdot_nt(a, b):
    return lax.dot_general(a.astype(BF16), b.astype(BF16), (((1,), (1,)), ((), ())),
                           preferred_element_type=F32)


def _bdot_tn(a, b):
    return lax.dot_general(a.astype(BF16), b.astype(BF16), (((0,), (0,)), ((), ())),
                           preferred_element_type=F32)


def _sigmoid(x):
    return 1.0 / (1.0 + jnp.exp(-x))


def _silu(x):
    return x * _sigmoid(x)


def _rms_scale(x):
    return lax.rsqrt(jnp.mean(x * x, axis=-1, keepdims=True) + RMS_EPS)


def _params(n_grid_dims):
    return pltpu.CompilerParams(dimension_semantics=("arbitrary",) * n_grid_dims,
                                vmem_limit_bytes=VMEM_LIMIT_BYTES)


def _in_proj_kernel(x_ref, nw_ref, w_ref, wba_ref, p_ref, ba_ref, h_ref):
    @pl.when(pl.program_id(1) == 0)
    def _():
        x = x_ref[...]
        h = (x * _rms_scale(x) * nw_ref[...]).astype(BF16)
        h_ref[...] = h
        ba_ref[...] = jnp.dot(h, wba_ref[...], preferred_element_type=F32)

    p_ref[...] = jnp.dot(h_ref[...], w_ref[...], preferred_element_type=F32).astype(BF16)


def _in_proj(x2d, norm_w, w_main, w_ba):
    t, d = x2d.shape
    tm = min(1024, t)
    tn = 1024
    return pl.pallas_call(
        _in_proj_kernel,
        grid=(t // tm, P_WIDTH // tn),
        in_specs=[
            pl.BlockSpec((tm, d), lambda i, j: (i, 0)),
            pl.BlockSpec((1, d), lambda i, j: (0, 0)),
            pl.BlockSpec((d, tn), lambda i, j: (0, j)),
            pl.BlockSpec((d, BA_WIDTH), lambda i, j: (0, 0)),
        ],
        out_specs=[
            pl.BlockSpec((tm, tn), lambda i, j: (i, j)),
            pl.BlockSpec((tm, BA_WIDTH), lambda i, j: (i, 0)),
        ],
        out_shape=[
            jax.ShapeDtypeStruct((t, P_WIDTH), BF16),
            jax.ShapeDtypeStruct((t, BA_WIDTH), F32),
        ],
        scratch_shapes=[pltpu.VMEM((tm, d), BF16)],
        compiler_params=_params(2),
        name="in_proj",
    )(x2d, norm_w, w_main, w_ba)


def _tri_inv(a, c):
    r = lax.broadcasted_iota(jnp.int32, (c, c), 0)
    q = lax.broadcasted_iota(jnp.int32, (c, c), 1)
    shift = int(math.log2(TRI_BLOCK))
    same = (r >> shift) == (q >> shift)
    d = jnp.where(same, a, 0.0)
    p = jnp.where(r == q, 1.0, 0.0) - d
    pw = d
    for _ in range(shift - 1):
        pw = _bdot(pw, pw)
        p = p + _bdot(p, pw)
    blk = TRI_BLOCK
    while blk < c:
        shift += 1
        same_next = (r >> shift) == (q >> shift)
        b = jnp.where(jnp.logical_and(same_next, jnp.logical_not(same)), a, 0.0)
        p = p - _bdot(_bdot(p, b), p)
        same = same_next
        blk *= 2
    return p


def _gdn_kernel(qkv_ref, z_ref, ba_ref, convw_ref, alog_ref, dtb_ref, gnw_ref, conv0_ref, s0_ref,
                o_ref, s_ref, xbuf_ref, *, c):
    @pl.when(pl.program_id(1) == 0)
    def _():
        xbuf_ref[0:SUBLANES, :] = conv0_ref[0]
        s_ref[...] = s0_ref[...]

    x_new = qkv_ref[...].astype(F32)
    xbuf_ref[SUBLANES:SUBLANES + c, :] = x_new
    w = convw_ref[...]
    y = xbuf_ref[5:5 + c, :] * w[0:1, :]
    y = y + xbuf_ref[6:6 + c, :] * w[1:2, :]
    y = y + xbuf_ref[7:7 + c, :] * w[2:3, :]
    y = y + x_new * w[3:4, :]
    xbuf_ref[0:SUBLANES, :] = xbuf_ref[c:c + SUBLANES, :]
    y = _silu(y)

    beta = _sigmoid(ba_ref[:, 0:LANES])
    sp_in = ba_ref[:, LANES:2 * LANES] + dtb_ref[...]
    softplus = jnp.maximum(sp_in, 0.0) + jnp.log(1.0 + jnp.exp(-jnp.abs(sp_in)))
    g = -jnp.exp(alog_ref[...]) * softplus
    r = lax.broadcasted_iota(jnp.int32, (c, c), 0)
    q = lax.broadcasted_iota(jnp.int32, (c, c), 1)
    causal = r >= q
    strict = r > q
    gc = jnp.dot(jnp.where(causal, 1.0, 0.0), g, precision=lax.Precision.HIGHEST,
                 preferred_element_type=F32)
    gc_t = jnp.concatenate([gc, jnp.zeros((LANES - c, LANES), F32)], axis=0).T
    g_last = gc[c - 1:c, :]
    e_gc = jnp.exp(gc)
    e_kd = jnp.exp(g_last - gc)
    e_last = jnp.exp(g_last)
    beta_e = beta * e_gc

    gnw = gnw_ref[...]
    for h in range(GDN_HEADS):
        lo, hi = h * GDN_DK, (h + 1) * GDN_DK
        qh = y[:, lo:hi]
        kh = y[:, GDN_QK + lo:GDN_QK + hi]
        vh = y[:, 2 * GDN_QK + lo:2 * GDN_QK + hi]
        qn = qh * lax.rsqrt(jnp.sum(qh * qh, axis=-1, keepdims=True) + L2_EPS) * (GDN_DK ** -0.5)
        kn = kh * lax.rsqrt(jnp.sum(kh * kh, axis=-1, keepdims=True) + L2_EPS)
        beta_c = beta[:, h:h + 1]
        diff = gc[:, h:h + 1] - gc_t[h:h + 1, 0:c]
        dmat = jnp.where(causal, jnp.exp(jnp.where(causal, diff, 0.0)), 0.0)
        kb = kn * beta_c
        qk = _bdot_nt(jnp.concatenate([qn, kb], axis=0), kn)
        attn = qk[0:c, :] * dmat
        a = jnp.where(strict, qk[c:2 * c, :] * dmat, 0.0)
        t_inv = _tri_inv(a, c)
        rhs = jnp.concatenate([vh * beta_c, kn * beta_e[:, h:h + 1]], axis=1)
        w_u = _bdot(t_inv, rhs)
        u0 = w_u[:, 0:GDN_DV]
        kcd = w_u[:, GDN_DV:2 * GDN_DV]
        qd = qn * e_gc[:, h:h + 1]
        kd = kn * e_kd[:, h:h + 1]
        s = s_ref[0, h]
        x_s = _bdot(jnp.concatenate([kcd, qd], axis=0), s)
        u = u0 - x_s[0:c, :]
        o = x_s[c:2 * c, :] + _bdot(attn, u)
        s_ref[0, h] = s * e_last[:, h:h + 1] + _bdot_tn(kd, u)
        o = o * _rms_scale(o) * gnw
        o = o * _silu(z_ref[:, lo:hi].astype(F32))
        o_ref[:, lo:hi] = o.astype(BF16)


def _gdn_mixer(p, ba, conv_w, alog_row, dtb_row, gnw_row, conv0, s0, batch, seq):
    c = min(CHUNK, seq)
    n = seq // c
    row = lambda b, s: (b * n + s, 0)
    return pl.pallas_call(
        functools.partial(_gdn_kernel, c=c),
        grid=(batch, n),
        in_specs=[
            pl.BlockSpec((c, GDN_CONV_DIM), row),
            pl.BlockSpec((c, GDN_V), lambda b, s: (b * n + s, GDN_CONV_DIM // GDN_V)),
            pl.BlockSpec((c, BA_WIDTH), row),
            pl.BlockSpec((CONV_W, GDN_CONV_DIM), lambda b, s: (0, 0)),
            pl.BlockSpec((1, LANES), lambda b, s: (0, 0)),
            pl.BlockSpec((1, LANES), lambda b, s: (0, 0)),
            pl.BlockSpec((1, GDN_DV), lambda b, s: (0, 0)),
            pl.BlockSpec((1, SUBLANES, GDN_CONV_DIM), lambda b, s: (b, 0, 0)),
            pl.BlockSpec((1, GDN_HEADS, GDN_DK, GDN_DV), lambda b, s: (b, 0, 0, 0)),
        ],
        out_specs=[
            pl.BlockSpec((c, GDN_V), row),
            pl.BlockSpec((1, GDN_HEADS, GDN_DK, GDN_DV), lambda b, s: (b, 0, 0, 0)),
        ],
        out_shape=[
            jax.ShapeDtypeStruct((batch * seq, GDN_V), BF16),
            jax.ShapeDtypeStruct((batch, GDN_HEADS, GDN_DK, GDN_DV), F32),
        ],
        scratch_shapes=[pltpu.VMEM((SUBLANES + c, GDN_CONV_DIM), F32)],
        compiler_params=_params(2),
        name="gdn_mixer",
    )(p, p, ba, conv_w, alog_row, dtb_row, gnw_row, conv0, s0)


def _rope_kernel(inv_ref, cos_ref, sin_ref, *, rows, offset):
    pos = lax.broadcasted_iota(jnp.int32, (rows, LANES), 0) + (pl.program_id(0) * rows + offset)
    ang = pos.astype(F32) * inv_ref[...]
    cos_ref[...] = jnp.cos(ang)
    sin_ref[...] = jnp.sin(ang)


def _rope_tables(inv_row, seq, offset):
    rows = min(512, seq)
    return pl.pallas_call(
        functools.partial(_rope_kernel, rows=rows, offset=offset),
        grid=(seq // rows,),
        in_specs=[pl.BlockSpec((1, LANES), lambda i: (0, 0))],
        out_specs=[pl.BlockSpec((rows, LANES), lambda i: (i, 0))] * 2,
        out_shape=[jax.ShapeDtypeStruct((seq, LANES), F32)] * 2,
        compiler_params=_params(1),
        name="rope_tables",
    )(inv_row)


def _ret_kernel(q_ref, k_ref, v_ref, g_ref, cos_ref, sin_ref, gnw_ref, s0_ref, o_ref, s_ref, *, c):
    @pl.when(pl.program_id(1) == 0)
    def _():
        s_ref[...] = s0_ref[...]

    cos = cos_ref[...]
    sin = sin_ref[...]
    r = lax.broadcasted_iota(jnp.int32, (c, c), 0)
    q_i = lax.broadcasted_iota(jnp.int32, (c, c), 1)
    causal = r >= q_i
    dpos = (r - q_i).astype(F32)
    tok = lax.broadcasted_iota(jnp.int32, (c, 1), 0).astype(F32)
    half = RET_DK // 2

    def rotate(x):
        xe, xo = x[:, 0:half], x[:, half:RET_DK]
        return jnp.concatenate([xe * cos - xo * sin, xo * cos + xe * sin], axis=1)

    for h in range(RET_HEADS):
        lo, hi = h * RET_DK, (h + 1) * RET_DK
        lg = math.log(1.0 - 2.0 ** (-5.0 - h))
        qr = rotate(q_ref[:, lo:hi].astype(F32))
        kr = rotate(k_ref[:, lo:hi].astype(F32)) * (RET_DK ** -0.5)
        v = v_ref[:, lo:hi]
        dmat = jnp.where(causal, jnp.exp(jnp.where(causal, dpos * lg, 0.0)), 0.0)
        attn = _bdot_nt(qr, kr) * dmat
        qd = qr * jnp.exp((tok + 1.0) * lg)
        kd = kr * jnp.exp((c - 1.0 - tok) * lg)
        s = s_ref[0, h]
        o = _bdot(qd, s) + _bdot(attn, v)
        s_ref[0, h] = s * math.exp(c * lg) + _bdot_tn(kd, v)
        mu = jnp.mean(o, axis=-1, keepdims=True)
        var = jnp.mean(jnp.square(o - mu), axis=-1, keepdims=True)
        o = (o - mu) * lax.rsqrt(var + GN_EPS) * gnw_ref[:, lo:hi]
        o = o * _silu(g_ref[:, lo:hi].astype(F32))
        o_ref[:, lo:hi] = o.astype(BF16)


def _ret_mixer(p, cos, sin, gnw_row, s0, batch, seq):
    c = min(CHUNK, seq)
    n = seq // c
    col = lambda j: (lambda b, s: (b * n + s, j))
    return pl.pallas_call(
        functools.partial(_ret_kernel, c=c),
        grid=(batch, n),
        in_specs=[
            pl.BlockSpec((c, RET_QK), col(4)),
            pl.BlockSpec((c, RET_QK), col(5)),
            pl.BlockSpec((c, RET_V), col(6)),
            pl.BlockSpec((c, RET_V), col(7)),
            pl.BlockSpec((c, LANES), lambda b, s: (s, 0)),
            pl.BlockSpec((c, LANES), lambda b, s: (s, 0)),
            pl.BlockSpec((1, RET_V), lambda b, s: (0, 0)),
            pl.BlockSpec((1, RET_HEADS, RET_DK, RET_DV), lambda b, s: (b, 0, 0, 0)),
        ],
        out_specs=[
            pl.BlockSpec((c, RET_V), lambda b, s: (b * n + s, 0)),
            pl.BlockSpec((1, RET_HEADS, RET_DK, RET_DV), lambda b, s: (b, 0, 0, 0)),
        ],
        out_shape=[
            jax.ShapeDtypeStruct((batch * seq, RET_V), BF16),
            jax.ShapeDtypeStruct((batch, RET_HEADS, RET_DK, RET_DV), F32),
        ],
        compiler_params=_params(2),
        name="ret_mixer",
    )(p, p, p, p, cos, sin, gnw_row, s0)


def _out_proj_kernel(x_ref, ma_ref, mr_ref, woa_ref, wor_ref, nw_ref, x1_ref, h2_ref):
    x1 = x_ref[...] + jnp.dot(ma_ref[...], woa_ref[...], preferred_element_type=F32)
    x1 = x1 + jnp.dot(mr_ref[...], wor_ref[...], preferred_element_type=F32)
    x1_ref[...] = x1
    h2_ref[...] = (x1 * _rms_scale(x1) * nw_ref[...]).astype(BF16)


def _out_proj(x2d, mix_a, mix_r, w_out_a, w_out_r, norm_w):
    t, d = x2d.shape
    tm = min(512, t)
    return pl.pallas_call(
        _out_proj_kernel,
        grid=(t // tm,),
        in_specs=[
            pl.BlockSpec((tm, d), lambda i: (i, 0)),
            pl.BlockSpec((tm, GDN_V), lambda i: (i, 0)),
            pl.BlockSpec((tm, RET_V), lambda i: (i, 0)),
            pl.BlockSpec((GDN_V, d), lambda i: (0, 0)),
            pl.BlockSpec((RET_V, d), lambda i: (0, 0)),
            pl.BlockSpec((1, d), lambda i: (0, 0)),
        ],
        out_specs=[
            pl.BlockSpec((tm, d), lambda i: (i, 0)),
            pl.BlockSpec((tm, d), lambda i: (i, 0)),
        ],
        out_shape=[
            jax.ShapeDtypeStruct((t, d), F32),
            jax.ShapeDtypeStruct((t, d), BF16),
        ],
        compiler_params=_params(1),
        name="out_proj",
    )(x2d, mix_a, mix_r, w_out_a, w_out_r, norm_w)


def _ffn_kernel(h2_ref, x1_ref, wg_ref, wu_ref, wd_ref, fw_ref, y_ref, acc_ref, *, n_ff, final_norm):
    f = pl.program_id(1)

    @pl.when(f == 0)
    def _():
        acc_ref[...] = x1_ref[...]

    h = h2_ref[...]
    gate = jnp.dot(h, wg_ref[...], preferred_element_type=F32)
    up = jnp.dot(h, wu_ref[...], preferred_element_type=F32)
    act = (_silu(gate) * up).astype(BF16)
    acc_ref[...] += jnp.dot(act, wd_ref[...], preferred_element_type=F32)

    @pl.when(f == n_ff - 1)
    def _():
        x2 = acc_ref[...]
        if final_norm:
            x2 = x2 * _rms_scale(x2) * fw_ref[...]
        y_ref[...] = x2


def _ffn(h2, x1, w_gate_up, w_down, final_w, final_norm):
    t, d = x1.shape
    d_ff = w_down.shape[0]
    tm = min(512, t)
    tf = 512
    n_ff = d_ff // tf
    return pl.pallas_call(
        functools.partial(_ffn_kernel, n_ff=n_ff, final_norm=final_norm),
        grid=(t // tm, n_ff),
        in_specs=[
            pl.BlockSpec((tm, d), lambda i, f: (i, 0)),
            pl.BlockSpec((tm, d), lambda i, f: (i, 0)),
            pl.BlockSpec((d, tf), lambda i, f: (0, f)),
            pl.BlockSpec((d, tf), lambda i, f: (0, n_ff + f)),
            pl.BlockSpec((tf, d), lambda i, f: (f, 0)),
            pl.BlockSpec((1, d), lambda i, f: (0, 0)),
        ],
        out_specs=pl.BlockSpec((tm, d), lambda i, f: (i, 0)),
        out_shape=jax.ShapeDtypeStruct((t, d), F32),
        scratch_shapes=[pltpu.VMEM((tm, d), F32)],
        compiler_params=_params(2),
        name="ffn",
    )(h2, x1, w_gate_up, w_gate_up, w_down, final_w)


def _pair_perm():
    idx = np.concatenate([np.arange(0, RET_DK, 2), np.arange(1, RET_DK, 2)])
    return idx


def _prep_layer_weights(w_in, w_out, w_gate_up, w_down, a_log, dt_bias, gdn_norm_w, ret_gn_w,
                        attn_norm_w, ffn_norm_w, conv_w):
    d = w_in.shape[0]
    perm = _pair_perm()
    head_perm = np.concatenate([h * RET_DK + perm for h in range(RET_HEADS)])
    o_b = GDN_CONV_DIM + GDN_V
    o_qr = o_b + 2 * GDN_HEADS
    w_main = jnp.concatenate([
        w_in[:, 0:o_b],
        w_in[:, o_qr:o_qr + RET_QK][:, head_perm],
        w_in[:, o_qr + RET_QK:o_qr + 2 * RET_QK][:, head_perm],
        w_in[:, o_qr + 2 * RET_QK:],
    ], axis=1).astype(BF16)
    pad = jnp.zeros((d, LANES - GDN_HEADS), w_in.dtype)
    w_ba = jnp.concatenate([w_in[:, o_b:o_b + GDN_HEADS], pad,
                            w_in[:, o_b + GDN_HEADS:o_qr], pad], axis=1).astype(BF16)
    row_pad = lambda v: jnp.pad(v.astype(F32), (0, LANES - v.shape[0])).reshape(1, LANES)
    return dict(
        w_main=w_main, w_ba=w_ba,
        w_out_a=w_out[0:GDN_V].astype(BF16), w_out_r=w_out[GDN_V:].astype(BF16),
        w_gate_up=w_gate_up.astype(BF16), w_down=w_down.astype(BF16),
        alog_row=row_pad(a_log), dtb_row=row_pad(dt_bias),
        gnw_row=gdn_norm_w.reshape(1, GDN_DV).astype(F32),
        ret_gnw_row=ret_gn_w.reshape(1, RET_V).astype(F32),
        attn_norm_w=attn_norm_w.reshape(1, d), ffn_norm_w=ffn_norm_w.reshape(1, d),
        conv_w=conv_w,
    )


def _layer(x2d, batch, seq, cos, sin, conv_state, s_gdn, s_ret, lw, final_w, final_norm):
    perm = _pair_perm()
    inv_perm = np.argsort(perm)
    p, ba = _in_proj(x2d, lw["attn_norm_w"], lw["w_main"], lw["w_ba"])
    conv0 = jnp.pad(conv_state.astype(F32), ((0, 0), (SUBLANES - (CONV_W - 1), 0), (0, 0)))
    mix_a, s_gdn_new = _gdn_mixer(p, ba, lw["conv_w"], lw["alog_row"], lw["dtb_row"], lw["gnw_row"],
                                  conv0, s_gdn.astype(F32), batch, seq)
    mix_r, s_ret_new = _ret_mixer(p, cos, sin, lw["ret_gnw_row"], s_ret.astype(F32)[:, :, perm, :],
                                  batch, seq)
    x1, h2 = _out_proj(x2d, mix_a, mix_r, lw["w_out_a"], lw["w_out_r"], lw["ffn_norm_w"])
    y = _ffn(h2, x1, lw["w_gate_up"], lw["w_down"], final_w, final_norm)
    rows = p.reshape(batch, seq, P_WIDTH)[:, seq - (CONV_W - 1):, 0:GDN_CONV_DIM].astype(conv_state.dtype)
    return y, rows, s_gdn_new.astype(s_gdn.dtype), s_ret_new[:, :, inv_perm, :].astype(s_ret.dtype)


def kernel(x_prompt, x_sample, state_gdn_conv, state_gdn, state_ret, attn_norm_w, w_in, conv_w, a_log, dt_bias, gdn_norm_w, ret_gn_w, w_out, ffn_norm_w, w_gate_up, w_down, final_norm_w):
    depth = w_in.shape[0]
    bp, lp, d = x_prompt.shape
    bs, ls, _ = x_sample.shape
    assert lp >= CONV_W - 1 and ls >= CONV_W - 1
    inv = 1.0 / (ROPE_BASE ** jnp.linspace(0.0, 1.0, RET_DK // 2, dtype=F32))
    inv_row = inv.reshape(1, LANES)
    cos_p, sin_p = _rope_tables(inv_row, lp, 0)
    cos_s, sin_s = _rope_tables(inv_row, ls, PAST_LEN)
    final_w = final_norm_w.reshape(1, d)
    hp = x_prompt.reshape(bp * lp, d)
    hs = x_sample.reshape(bs * ls, d)
    outs = [[] for _ in range(6)]
    for l in range(depth):
        lw = _prep_layer_weights(w_in[l], w_out[l], w_gate_up[l], w_down[l], a_log[l], dt_bias[l],
                                 gdn_norm_w[l], ret_gn_w[l], attn_norm_w[l], ffn_norm_w[l], conv_w[l])
        last = l == depth - 1
        conv0 = jnp.zeros((bp, CONV_W - 1, GDN_CONV_DIM), x_prompt.dtype)
        sg0 = jnp.zeros((bp, GDN_HEADS, GDN_DK, GDN_DV), state_gdn.dtype)
        sr0 = jnp.zeros((bp, RET_HEADS, RET_DK, RET_DV), state_ret.dtype)
        hp, c_p, g_p, r_p = _layer(hp, bp, lp, cos_p, sin_p, conv0, sg0, sr0, lw, final_w, last)
        hs, c_s, g_s, r_s = _layer(hs, bs, ls, cos_s, sin_s, state_gdn_conv[l], state_gdn[l],
                                   state_ret[l], lw, final_w, last)
        for acc, val in zip(outs, (c_p, g_p, r_p, c_s, g_s, r_s)):
            acc.append(val)
    return (hp.reshape(bp, lp, d), hs.reshape(bs, ls, d)) + tuple(jnp.stack(o) for o in outs)
```

```python
import functools
import math

import jax
import jax.numpy as jnp
import numpy as np
from jax import lax
from jax.experimental import pallas as pl
from jax.experimental.pallas import tpu as pltpu

F32 = jnp.float32
BF16 = jnp.bfloat16

GDN_HEADS = 8
GDN_DK = 128
GDN_DV = 128
CONV_W = 4
RET_HEADS = 4
RET_DK = 256
RET_DV = 256
GDN_QK = GDN_HEADS * GDN_DK
GDN_V = GDN_HEADS * GDN_DV
GDN_CONV_DIM = 2 * GDN_QK + GDN_V
RET_QK = RET_HEADS * RET_DK
RET_V = RET_HEADS * RET_DV
CHUNK = 64
PAST_LEN = 4096
RMS_EPS = 1e-6
GN_EPS = 1e-5
L2_EPS = 1e-6
ROPE_BASE = 10000.0

LANES = 128
SUBLANES = 8
V7X_VMEM_BYTES = 64 * 1024 * 1024
VMEM_LIMIT_BYTES = 52 * 1024 * 1024
TRI_BLOCK = 16

P_WIDTH = 8 * 1024
BA_WIDTH = 2 * LANES


def _bdot(a, b):
    return jnp.dot(a.astype(BF16), b.astype(BF16), preferred_element_type=F32)


def _bdot_nt(a, b):
    return lax.dot_general(a.astype(BF16), b.astype(BF16), (((1,), (1,)), ((), ())),
                           preferred_element_type=F32)


def _bdot_tn(a, b):
    return lax.dot_general(a.astype(BF16), b.astype(BF16), (((0,), (0,)), ((), ())),
                           preferred_element_type=F32)


def _sigmoid(x):
    return 1.0 / (1.0 + jnp.exp(-x))


def _silu(x):
    return x * _sigmoid(x)


def _rms_scale(x):
    return lax.rsqrt(jnp.mean(x * x, axis=-1, keepdims=True) + RMS_EPS)


def _params(n_grid_dims):
    return pltpu.CompilerParams(dimension_semantics=("arbitrary",) * n_grid_dims,
                                vmem_limit_bytes=VMEM_LIMIT_BYTES)


def _in_proj_kernel(x_ref, nw_ref, w_ref, wba_ref, p_ref, ba_ref, h_ref):
    @pl.when(pl.program_id(1) == 0)
    def _():
        x = x_ref[...]
        h = (x * _rms_scale(x) * nw_ref[...]).astype(BF16)
        h_ref[...] = h
        ba_ref[...] = jnp.dot(h, wba_ref[...], preferred_element_type=F32)

    p_ref[...] = jnp.dot(h_ref[...], w_ref[...], preferred_element_type=F32).astype(BF16)


def _in_proj(x2d, norm_w, w_main, w_ba):
    t, d = x2d.shape
    tm = min(1024, t)
    tn = 1024
    return pl.pallas_call(
        _in_proj_kernel,
        grid=(t // tm, P_WIDTH // tn),
        in_specs=[
            pl.BlockSpec((tm, d), lambda i, j: (i, 0)),
            pl.BlockSpec((1, d), lambda i, j: (0, 0)),
            pl.BlockSpec((d, tn), lambda i, j: (0, j)),
            pl.BlockSpec((d, BA_WIDTH), lambda i, j: (0, 0)),
        ],
        out_specs=[
            pl.BlockSpec((tm, tn), lambda i, j: (i, j)),
            pl.BlockSpec((tm, BA_WIDTH), lambda i, j: (i, 0)),
        ],
        out_shape=[
            jax.ShapeDtypeStruct((t, P_WIDTH), BF16),
            jax.ShapeDtypeStruct((t, BA_WIDTH), F32),
        ],
        scratch_shapes=[pltpu.VMEM((tm, d), BF16)],
        compiler_params=_params(2),
        name="in_proj",
    )(x2d, norm_w, w_main, w_ba)


M_CAUSAL, M_STRICT, M_EYE, M_DIAG, M_MERGE0 = 0, 1, 2, 3, 4
CARRY_ROWS = 16
GDN_SEQS_PER_STEP = 4


def _gdn_constants(c):
    r = np.arange(c)[:, None]
    q = np.arange(c)[None, :]
    masks = [r >= q, r > q, r == q, (r // TRI_BLOCK) == (q // TRI_BLOCK)]
    blk = TRI_BLOCK
    while blk < c:
        masks.append(((r // (2 * blk)) == (q // (2 * blk))) & ((r // blk) != (q // blk)))
        blk *= 2
    masks = np.stack(masks).astype(np.float32)
    shift = np.zeros(((CONV_W - 1) * c, CARRY_ROWS + c), np.float32)
    for k in range(CONV_W - 1):
        shift[k * c + np.arange(c), CARRY_ROWS + np.arange(c) - (CONV_W - 1) + k] = 1.0
    return jnp.asarray(masks), jnp.asarray(shift, dtype=BF16)


def _tri_inv(a_list, masks_ref, c):
    pw = [a * masks_ref[M_DIAG] for a in a_list]
    p = [masks_ref[M_EYE] - d for d in pw]
    for _ in range(int(math.log2(TRI_BLOCK)) - 1):
        pw = [_bdot(x, x) for x in pw]
        p = [x + _bdot(x, y) for x, y in zip(p, pw)]
    blk, level = TRI_BLOCK, M_MERGE0
    while blk < c:
        t = [_bdot(x, a * masks_ref[level]) for x, a in zip(p, a_list)]
        p = [x - _bdot(y, x) for x, y in zip(p, t)]
        blk, level = 2 * blk, level + 1
    return p


def _gdn_kernel(qkv_ref, z_ref, ba_ref, convw_ref, alog_ref, dtb_ref, gnw_ref, conv0_ref, s0_ref,
                masks_ref, shift_ref, o_ref, s_ref, xbuf_ref, *, c, bb):
    @pl.when(pl.program_id(1) == 0)
    def _():
        xbuf_ref[:, 0:CARRY_ROWS, :] = conv0_ref[...]
        s_ref[...] = s0_ref[...]

    heads = range(GDN_HEADS)
    col = lambda arr, h: arr[:, h:h + 1]
    lanes = lambda base, h: slice(base + h * GDN_DK, base + (h + 1) * GDN_DK)
    rows = lambda arr, i: arr[i * c:(i + 1) * c, :]
    w = convw_ref[...]

    def per_sequence(i):
        xbuf_ref[i, CARRY_ROWS:CARRY_ROWS + c, :] = qkv_ref[i]
        shifted = jnp.dot(shift_ref[...], xbuf_ref[i], preferred_element_type=F32)
        y = shifted[0:c, :] * w[0:1, :]
        y = y + shifted[c:2 * c, :] * w[1:2, :]
        y = y + shifted[2 * c:3 * c, :] * w[2:3, :]
        y = y + qkv_ref[i].astype(F32) * w[3:4, :]
        xbuf_ref[i, 0:CARRY_ROWS, :] = xbuf_ref[i, c:c + CARRY_ROWS, :]
        y = _silu(y)

        beta = _sigmoid(ba_ref[i, :, 0:LANES])
        sp_in = ba_ref[i, :, LANES:2 * LANES] + dtb_ref[...]
        softplus = jnp.maximum(sp_in, 0.0) + jnp.log(1.0 + jnp.exp(-jnp.abs(sp_in)))
        g = -jnp.exp(alog_ref[...]) * softplus
        gc = jnp.dot(masks_ref[M_CAUSAL], g, precision=lax.Precision.HIGHEST,
                     preferred_element_type=F32)
        gc_t = jnp.concatenate([gc, jnp.zeros((LANES - c, LANES), F32)], axis=0).T
        g_last = gc[c - 1:c, :]
        e_gc = jnp.exp(gc)

        def inv_norm(base, fill):
            sq = jnp.concatenate([jnp.square(y[:, lanes(base, h)]) for h in heads], axis=0)
            tot = _bdot(sq, jnp.full((GDN_DK, LANES), fill, F32))
            return lax.rsqrt(tot + fill * L2_EPS)

        return dict(y=y, beta=beta, gc=gc, gc_t=gc_t, e_gc=e_gc, e_kd=jnp.exp(g_last - gc),
                    e_last=jnp.exp(g_last), beta_e=beta * e_gc,
                    inv_q=inv_norm(0, float(GDN_DK)), inv_k=inv_norm(GDN_QK, 1.0))

    seqs = [per_sequence(i) for i in range(bb)]
    probs = [(i, h) for i in range(bb) for h in heads]
    qn = [seqs[i]["y"][:, lanes(0, h)] * rows(seqs[i]["inv_q"], h) for i, h in probs]
    kn = [seqs[i]["y"][:, lanes(GDN_QK, h)] * rows(seqs[i]["inv_k"], h) for i, h in probs]
    kb = [kn[j] * col(seqs[i]["beta"], h) for j, (i, h) in enumerate(probs)]
    qk = [_bdot_nt(jnp.concatenate([qn[j], kb[j]], axis=0), kn[j]) for j in range(len(probs))]
    attn, a_mat = [], []
    for j, (i, h) in enumerate(probs):
        diff = col(seqs[i]["gc"], h) - seqs[i]["gc_t"][h:h + 1, 0:c]
        decay = jnp.exp(jnp.minimum(diff, 0.0))
        attn.append(qk[j][0:c, :] * (decay * masks_ref[M_CAUSAL]))
        a_mat.append(qk[j][c:2 * c, :] * (decay * masks_ref[M_STRICT]))
    t_inv = _tri_inv(a_mat, masks_ref, c)
    w_u = [_bdot(t_inv[j], jnp.concatenate([seqs[i]["y"][:, lanes(2 * GDN_QK, h)] * col(seqs[i]["beta"], h),
                                            kn[j] * col(seqs[i]["beta_e"], h)], axis=1))
           for j, (i, h) in enumerate(probs)]
    x_s = [_bdot(jnp.concatenate([w_u[j][:, GDN_DV:2 * GDN_DV], qn[j] * col(seqs[i]["e_gc"], h)], axis=0),
                 s_ref[i, h]) for j, (i, h) in enumerate(probs)]
    u = [w_u[j][:, 0:GDN_DV] - x_s[j][0:c, :] for j in range(len(probs))]
    o = [x_s[j][c:2 * c, :] + _bdot(attn[j], u[j]) for j in range(len(probs))]
    ds = [_bdot_tn(kn[j] * col(seqs[i]["e_kd"], h), u[j]) for j, (i, h) in enumerate(probs)]
    gnw = gnw_ref[...]
    for j, (i, h) in enumerate(probs):
        s_ref[i, h] = s_ref[i, h] * col(seqs[i]["e_last"], h) + ds[j]
        out = o[j] * _rms_scale(o[j]) * gnw
        out = out * _silu(z_ref[i, :, lanes(0, h)].astype(F32))
        o_ref[i, :, lanes(0, h)] = out.astype(BF16)


def _gdn_mixer(p3, ba3, conv_w, alog_row, dtb_row, gnw_row, conv0, s0):
    batch, seq, _ = p3.shape
    c = min(CHUNK, seq)
    bb = min(GDN_SEQS_PER_STEP, batch)
    masks, shift = _gdn_constants(c)
    const = lambda shape: pl.BlockSpec(shape, lambda b, s: (0,) * len(shape))
    state_spec = pl.BlockSpec((bb, GDN_HEADS, GDN_DK, GDN_DV), lambda b, s: (b, 0, 0, 0))
    return pl.pallas_call(
        functools.partial(_gdn_kernel, c=c, bb=bb),
        grid=(batch // bb, seq // c),
        in_specs=[
            pl.BlockSpec((bb, c, GDN_CONV_DIM), lambda b, s: (b, s, 0)),
            pl.BlockSpec((bb, c, GDN_V), lambda b, s: (b, s, GDN_CONV_DIM // GDN_V)),
            pl.BlockSpec((bb, c, BA_WIDTH), lambda b, s: (b, s, 0)),
            const((CONV_W, GDN_CONV_DIM)),
            const((1, LANES)),
            const((1, LANES)),
            const((1, GDN_DV)),
            pl.BlockSpec((bb, CARRY_ROWS, GDN_CONV_DIM), lambda b, s: (b, 0, 0)),
            state_spec,
            const(masks.shape),
            const(shift.shape),
        ],
        out_specs=[
            pl.BlockSpec((bb, c, GDN_V), lambda b, s: (b, s, 0)),
            state_spec,
        ],
        out_shape=[
            jax.ShapeDtypeStruct((batch, seq, GDN_V), BF16),
            jax.ShapeDtypeStruct((batch, GDN_HEADS, GDN_DK, GDN_DV), F32),
        ],
        scratch_shapes=[pltpu.VMEM((bb, CARRY_ROWS + c, GDN_CONV_DIM), BF16)],
        compiler_params=_params(2),
        name="gdn_mixer",
    )(p3, p3, ba3, conv_w, alog_row, dtb_row, gnw_row, conv0, s0, masks, shift)


def _rope_kernel(inv_ref, cos_ref, sin_ref, *, rows, offset):
    pos = lax.broadcasted_iota(jnp.int32, (rows, LANES), 0) + (pl.program_id(0) * rows + offset)
    ang = pos.astype(F32) * inv_ref[...]
    cos_ref[...] = jnp.cos(ang)
    sin_ref[...] = jnp.sin(ang)


def _rope_tables(inv_row, seq, offset):
    rows = min(512, seq)
    return pl.pallas_call(
        functools.partial(_rope_kernel, rows=rows, offset=offset),
        grid=(seq // rows,),
        in_specs=[pl.BlockSpec((1, LANES), lambda i: (0, 0))],
        out_specs=[pl.BlockSpec((rows, LANES), lambda i: (i, 0))] * 2,
        out_shape=[jax.ShapeDtypeStruct((seq, LANES), F32)] * 2,
        compiler_params=_params(1),
        name="rope_tables",
    )(inv_row)


RET_CHUNK = 256
RET_SEQS_PER_STEP = 2


def _ret_kernel(q_ref, k_ref, v_ref, g_ref, cos_ref, sin_ref, gnw_ref, s0_ref, o_ref, s_ref,
                dmat_ref, scale_ref, *, c, bb):
    first_block = pl.program_id(1) == 0

    @pl.when(first_block)
    def _():
        s_ref[...] = s0_ref[...]

    @pl.when(jnp.logical_and(pl.program_id(0) == 0, first_block))
    def _():
        r = lax.broadcasted_iota(jnp.int32, (c, c), 0)
        q_i = lax.broadcasted_iota(jnp.int32, (c, c), 1)
        causal = r >= q_i
        dpos = (r - q_i).astype(F32)
        tok = lax.broadcasted_iota(jnp.int32, (c, LANES), 0).astype(F32)
        for h in range(RET_HEADS):
            lg = math.log(1.0 - 2.0 ** (-5.0 - h))
            dmat = jnp.where(causal, jnp.exp(jnp.where(causal, dpos * lg, 0.0)), 0.0)
            dmat_ref[h] = dmat * (RET_DK ** -0.5)
            scale_ref[0, h] = jnp.exp((tok + 1.0) * lg)
            scale_ref[1, h] = jnp.exp((c - 1.0 - tok) * lg) * (RET_DK ** -0.5)

    cos = cos_ref[...]
    sin = sin_ref[...]
    half = RET_DK // 2

    def rotate(x):
        xe, xo = x[:, 0:half], x[:, half:RET_DK]
        return jnp.concatenate([xe * cos - xo * sin, xo * cos + xe * sin], axis=1)

    both = lambda t: jnp.concatenate([t, t], axis=1)
    lanes = lambda h: slice(h * RET_DK, (h + 1) * RET_DK)
    probs = [(i, h) for i in range(bb) for h in range(RET_HEADS)]
    qr = [rotate(q_ref[i, :, lanes(h)].astype(F32)) for i, h in probs]
    kr = [rotate(k_ref[i, :, lanes(h)].astype(F32)) for i, h in probs]
    attn = [_bdot_nt(qr[j], kr[j]) * dmat_ref[h] for j, (i, h) in enumerate(probs)]
    o = [_bdot(qr[j] * both(scale_ref[0, h]), s_ref[i, h]) + _bdot(attn[j], v_ref[i, :, lanes(h)])
         for j, (i, h) in enumerate(probs)]
    ds = [_bdot_tn(kr[j] * both(scale_ref[1, h]), v_ref[i, :, lanes(h)]) for j, (i, h) in enumerate(probs)]
    for j, (i, h) in enumerate(probs):
        s_ref[i, h] = s_ref[i, h] * math.exp(c * math.log(1.0 - 2.0 ** (-5.0 - h))) + ds[j]
        mu = jnp.mean(o[j], axis=-1, keepdims=True)
        dev = o[j] - mu
        var = jnp.mean(jnp.square(dev), axis=-1, keepdims=True)
        out = dev * lax.rsqrt(var + GN_EPS) * gnw_ref[:, lanes(h)]
        out = out * _silu(g_ref[i, :, lanes(h)].astype(F32))
        o_ref[i, :, lanes(h)] = out.astype(BF16)


def _ret_mixer(p3, cos, sin, gnw_row, s0):
    batch, seq, _ = p3.shape
    c = min(RET_CHUNK, seq)
    bb = min(RET_SEQS_PER_STEP, batch)
    col = lambda j: pl.BlockSpec((bb, c, RET_QK), lambda b, s: (b, s, j))
    state_spec = pl.BlockSpec((bb, RET_HEADS, RET_DK, RET_DV), lambda b, s: (b, 0, 0, 0))
    return pl.pallas_call(
        functools.partial(_ret_kernel, c=c, bb=bb),
        grid=(batch // bb, seq // c),
        in_specs=[
            col(4), col(5), col(6), col(7),
            pl.BlockSpec((c, LANES), lambda b, s: (s, 0)),
            pl.BlockSpec((c, LANES), lambda b, s: (s, 0)),
            pl.BlockSpec((1, RET_V), lambda b, s: (0, 0)),
            state_spec,
        ],
        out_specs=[
            pl.BlockSpec((bb, c, RET_V), lambda b, s: (b, s, 0)),
            state_spec,
        ],
        out_shape=[
            jax.ShapeDtypeStruct((batch, seq, RET_V), BF16),
            jax.ShapeDtypeStruct((batch, RET_HEADS, RET_DK, RET_DV), F32),
        ],
        scratch_shapes=[pltpu.VMEM((RET_HEADS, c, c), F32), pltpu.VMEM((2, RET_HEADS, c, LANES), F32)],
        compiler_params=_params(2),
        name="ret_mixer",
    )(p3, p3, p3, p3, cos, sin, gnw_row, s0)


def _out_proj_kernel(x_ref, ma_ref, mr_ref, woa_ref, wor_ref, nw_ref, x1_ref, h2_ref):
    x1 = x_ref[...] + jnp.dot(ma_ref[...], woa_ref[...], preferred_element_type=F32)
    x1 = x1 + jnp.dot(mr_ref[...], wor_ref[...], preferred_element_type=F32)
    x1_ref[...] = x1
    h2_ref[...] = (x1 * _rms_scale(x1) * nw_ref[...]).astype(BF16)


def _out_proj(x2d, mix_a, mix_r, w_out_a, w_out_r, norm_w):
    t, d = x2d.shape
    tm = min(512, t)
    return pl.pallas_call(
        _out_proj_kernel,
        grid=(t // tm,),
        in_specs=[
            pl.BlockSpec((tm, d), lambda i: (i, 0)),
            pl.BlockSpec((tm, GDN_V), lambda i: (i, 0)),
            pl.BlockSpec((tm, RET_V), lambda i: (i, 0)),
            pl.BlockSpec((GDN_V, d), lambda i: (0, 0)),
            pl.BlockSpec((RET_V, d), lambda i: (0, 0)),
            pl.BlockSpec((1, d), lambda i: (0, 0)),
        ],
        out_specs=[
            pl.BlockSpec((tm, d), lambda i: (i, 0)),
            pl.BlockSpec((tm, d), lambda i: (i, 0)),
        ],
        out_shape=[
            jax.ShapeDtypeStruct((t, d), F32),
            jax.ShapeDtypeStruct((t, d), BF16),
        ],
        compiler_params=_params(1),
        name="out_proj",
    )(x2d, mix_a, mix_r, w_out_a, w_out_r, norm_w)


def _ffn_kernel(h2_ref, x1_ref, wg_ref, wu_ref, wd_ref, fw_ref, y_ref, acc_ref, *, n_ff, final_norm):
    f = pl.program_id(1)

    @pl.when(f == 0)
    def _():
        acc_ref[...] = x1_ref[...]

    h = h2_ref[...]
    gate = jnp.dot(h, wg_ref[...], preferred_element_type=F32)
    up = jnp.dot(h, wu_ref[...], preferred_element_type=F32)
    act = (_silu(gate) * up).astype(BF16)
    acc_ref[...] += jnp.dot(act, wd_ref[...], preferred_element_type=F32)

    @pl.when(f == n_ff - 1)
    def _():
        x2 = acc_ref[...]
        if final_norm:
            x2 = x2 * _rms_scale(x2) * fw_ref[...]
        y_ref[...] = x2


def _ffn(h2, x1, w_gate_up, w_down, final_w, final_norm):
    t, d = x1.shape
    d_ff = w_down.shape[0]
    tm = min(512, t)
    tf = 512
    n_ff = d_ff // tf
    return pl.pallas_call(
        functools.partial(_ffn_kernel, n_ff=n_ff, final_norm=final_norm),
        grid=(t // tm, n_ff),
        in_specs=[
            pl.BlockSpec((tm, d), lambda i, f: (i, 0)),
            pl.BlockSpec((tm, d), lambda i, f: (i, 0)),
            pl.BlockSpec((d, tf), lambda i, f: (0, f)),
            pl.BlockSpec((d, tf), lambda i, f: (0, n_ff + f)),
            pl.BlockSpec((tf, d), lambda i, f: (f, 0)),
            pl.BlockSpec((1, d), lambda i, f: (0, 0)),
        ],
        out_specs=pl.BlockSpec((tm, d), lambda i, f: (i, 0)),
        out_shape=jax.ShapeDtypeStruct((t, d), F32),
        scratch_shapes=[pltpu.VMEM((tm, d), F32)],
        compiler_params=_params(2),
        name="ffn",
    )(h2, x1, w_gate_up, w_gate_up, w_down, final_w)


def _pair_perm():
    idx = np.concatenate([np.arange(0, RET_DK, 2), np.arange(1, RET_DK, 2)])
    return idx


def _prep_layer_weights(w_in, w_out, w_gate_up, w_down, a_log, dt_bias, gdn_norm_w, ret_gn_w,
                        attn_norm_w, ffn_norm_w, conv_w):
    d = w_in.shape[0]
    perm = _pair_perm()
    head_perm = np.concatenate([h * RET_DK + perm for h in range(RET_HEADS)])
    o_b = GDN_CONV_DIM + GDN_V
    o_qr = o_b + 2 * GDN_HEADS
    w_main = jnp.concatenate([
        w_in[:, 0:o_b],
        w_in[:, o_qr:o_qr + RET_QK][:, head_perm],
        w_in[:, o_qr + RET_QK:o_qr + 2 * RET_QK][:, head_perm],
        w_in[:, o_qr + 2 * RET_QK:],
    ], axis=1).astype(BF16)
    pad = jnp.zeros((d, LANES - GDN_HEADS), w_in.dtype)
    w_ba = jnp.concatenate([w_in[:, o_b:o_b + GDN_HEADS], pad,
                            w_in[:, o_b + GDN_HEADS:o_qr], pad], axis=1).astype(BF16)
    row_pad = lambda v: jnp.pad(v.astype(F32), (0, LANES - v.shape[0])).reshape(1, LANES)
    return dict(
        w_main=w_main, w_ba=w_ba,
        w_out_a=w_out[0:GDN_V].astype(BF16), w_out_r=w_out[GDN_V:].astype(BF16),
        w_gate_up=w_gate_up.astype(BF16), w_down=w_down.astype(BF16),
        alog_row=row_pad(a_log), dtb_row=row_pad(dt_bias),
        gnw_row=gdn_norm_w.reshape(1, GDN_DV).astype(F32),
        ret_gnw_row=ret_gn_w.reshape(1, RET_V).astype(F32),
        attn_norm_w=attn_norm_w.reshape(1, d), ffn_norm_w=ffn_norm_w.reshape(1, d),
        conv_w=conv_w,
    )


def _layer(x2d, batch, seq, cos, sin, conv_state, s_gdn, s_ret, lw, final_w, final_norm):
    perm = _pair_perm()
    inv_perm = np.argsort(perm)
    p, ba = _in_proj(x2d, lw["attn_norm_w"], lw["w_main"], lw["w_ba"])
    conv0 = jnp.pad(conv_state.astype(BF16), ((0, 0), (CARRY_ROWS - (CONV_W - 1), 0), (0, 0)))
    p3 = p.reshape(batch, seq, P_WIDTH)
    mix_a, s_gdn_new = _gdn_mixer(p3, ba.reshape(batch, seq, BA_WIDTH), lw["conv_w"], lw["alog_row"],
                                  lw["dtb_row"], lw["gnw_row"], conv0, s_gdn.astype(F32))
    mix_r, s_ret_new = _ret_mixer(p3, cos, sin, lw["ret_gnw_row"], s_ret.astype(F32)[:, :, perm, :])
    mix_a = mix_a.reshape(batch * seq, GDN_V)
    mix_r = mix_r.reshape(batch * seq, RET_V)
    x1, h2 = _out_proj(x2d, mix_a, mix_r, lw["w_out_a"], lw["w_out_r"], lw["ffn_norm_w"])
    y = _ffn(h2, x1, lw["w_gate_up"], lw["w_down"], final_w, final_norm)
    rows = p3[:, seq - (CONV_W - 1):, 0:GDN_CONV_DIM].astype(conv_state.dtype)
    return y, rows, s_gdn_new.astype(s_gdn.dtype), s_ret_new[:, :, inv_perm, :].astype(s_ret.dtype)


def kernel(x_prompt, x_sample, state_gdn_conv, state_gdn, state_ret, attn_norm_w, w_in, conv_w, a_log, dt_bias, gdn_norm_w, ret_gn_w, w_out, ffn_norm_w, w_gate_up, w_down, final_norm_w):
    depth = w_in.shape[0]
    bp, lp, d = x_prompt.shape
    bs, ls, _ = x_sample.shape
    assert lp >= CONV_W - 1 and ls >= CONV_W - 1
    inv = 1.0 / (ROPE_BASE ** jnp.linspace(0.0, 1.0, RET_DK // 2, dtype=F32))
    inv_row = inv.reshape(1, LANES)
    cos_p, sin_p = _rope_tables(inv_row, lp, 0)
    cos_s, sin_s = _rope_tables(inv_row, ls, PAST_LEN)
    final_w = final_norm_w.reshape(1, d)
    hp = x_prompt.reshape(bp * lp, d)
    hs = x_sample.reshape(bs * ls, d)
    outs = [[] for _ in range(6)]
    for l in range(depth):
        lw = _prep_layer_weights(w_in[l], w_out[l], w_gate_up[l], w_down[l], a_log[l], dt_bias[l],
                                 gdn_norm_w[l], ret_gn_w[l], attn_norm_w[l], ffn_norm_w[l], conv_w[l])
        last = l == depth - 1
        conv0 = jnp.zeros((bp, CONV_W - 1, GDN_CONV_DIM), x_prompt.dtype)
        sg0 = jnp.zeros((bp, GDN_HEADS, GDN_DK, GDN_DV), state_gdn.dtype)
        sr0 = jnp.zeros((bp, RET_HEADS, RET_DK, RET_DV), state_ret.dtype)
        hp, c_p, g_p, r_p = _layer(hp, bp, lp, cos_p, sin_p, conv0, sg0, sr0, lw, final_w, last)
        hs, c_s, g_s, r_s = _layer(hs, bs, ls, cos_s, sin_s, state_gdn_conv[l], state_gdn[l],
                                   state_ret[l], lw, final_w, last)
        for acc, val in zip(outs, (c_p, g_p, r_p, c_s, g_s, r_s)):
            acc.append(val)
    return (hp.reshape(bp, lp, d), hs.reshape(bs, ls, d)) + tuple(jnp.stack(o) for o in outs)
```

```python
import functools
import math

import jax
import jax.numpy as jnp
import numpy as np
from jax import lax
from jax.experimental import pallas as pl
from jax.experimental.pallas import tpu as pltpu

F32 = jnp.float32
BF16 = jnp.bfloat16

GDN_HEADS = 8
GDN_DK = 128
GDN_DV = 128
CONV_W = 4
RET_HEADS = 4
RET_DK = 256
RET_DV = 256
GDN_QK = GDN_HEADS * GDN_DK
GDN_V = GDN_HEADS * GDN_DV
GDN_CONV_DIM = 2 * GDN_QK + GDN_V
RET_QK = RET_HEADS * RET_DK
RET_V = RET_HEADS * RET_DV
CHUNK = 64
PAST_LEN = 4096
RMS_EPS = 1e-6
GN_EPS = 1e-5
L2_EPS = 1e-6
ROPE_BASE = 10000.0

LANES = 128
SUBLANES = 8
V7X_VMEM_BYTES = 64 * 1024 * 1024
VMEM_LIMIT_BYTES = 52 * 1024 * 1024
VMEM_LIMIT_BIG_TILE_BYTES = 60 * 1024 * 1024
TRI_BLOCK = 16

P_WIDTH = 8 * 1024
BA_WIDTH = 2 * LANES

IN_PROJ_TM, IN_PROJ_TN = 1024, 2048
OUT_PROJ_TM = 512
FFN_TM, FFN_TF = 1024, 512


def _bdot(a, b):
    return jnp.dot(a.astype(BF16), b.astype(BF16), preferred_element_type=F32)


def _bdot_nt(a, b):
    return lax.dot_general(a.astype(BF16), b.astype(BF16), (((1,), (1,)), ((), ())),
                           preferred_element_type=F32)


def _bdot_tn(a, b):
    return lax.dot_general(a.astype(BF16), b.astype(BF16), (((0,), (0,)), ((), ())),
                           preferred_element_type=F32)


def _sigmoid(x):
    return 1.0 / (1.0 + jnp.exp(-x))


def _silu(x):
    return x * _sigmoid(x)


def _rms_scale(x):
    return lax.rsqrt(jnp.mean(x * x, axis=-1, keepdims=True) + RMS_EPS)


def _params(n_grid_dims, vmem_limit_bytes=VMEM_LIMIT_BYTES):
    return pltpu.CompilerParams(dimension_semantics=("arbitrary",) * n_grid_dims,
                                vmem_limit_bytes=vmem_limit_bytes)


def _in_proj_kernel(x_ref, nw_ref, w_ref, wba_ref, p_ref, ba_ref, h_ref):
    @pl.when(pl.program_id(1) == 0)
    def _():
        x = x_ref[...]
        h = (x * _rms_scale(x) * nw_ref[...]).astype(BF16)
        h_ref[...] = h
        ba_ref[...] = jnp.dot(h, wba_ref[...], preferred_element_type=F32)

    p_ref[...] = jnp.dot(h_ref[...], w_ref[...], preferred_element_type=F32).astype(BF16)


def _in_proj(x2d, norm_w, w_main, w_ba):
    t, d = x2d.shape
    tm = min(IN_PROJ_TM, t)
    tn = IN_PROJ_TN
    return pl.pallas_call(
        _in_proj_kernel,
        grid=(t // tm, P_WIDTH // tn),
        in_specs=[
            pl.BlockSpec((tm, d), lambda i, j: (i, 0)),
            pl.BlockSpec((1, d), lambda i, j: (0, 0)),
            pl.BlockSpec((d, tn), lambda i, j: (0, j)),
            pl.BlockSpec((d, BA_WIDTH), lambda i, j: (0, 0)),
        ],
        out_specs=[
            pl.BlockSpec((tm, tn), lambda i, j: (i, j)),
            pl.BlockSpec((tm, BA_WIDTH), lambda i, j: (i, 0)),
        ],
        out_shape=[
            jax.ShapeDtypeStruct((t, P_WIDTH), BF16),
            jax.ShapeDtypeStruct((t, BA_WIDTH), F32),
        ],
        scratch_shapes=[pltpu.VMEM((tm, d), BF16)],
        compiler_params=_params(2, VMEM_LIMIT_BIG_TILE_BYTES),
        name="in_proj",
    )(x2d, norm_w, w_main, w_ba)


M_CAUSAL, M_STRICT, M_EYE, M_DIAG, M_MERGE0 = 0, 1, 2, 3, 4
CARRY_ROWS = 16
GDN_SEQS_PER_STEP = 4


def _gdn_constants(c):
    r = np.arange(c)[:, None]
    q = np.arange(c)[None, :]
    masks = [r >= q, r > q, r == q, (r // TRI_BLOCK) == (q // TRI_BLOCK)]
    blk = TRI_BLOCK
    while blk < c:
        masks.append(((r // (2 * blk)) == (q // (2 * blk))) & ((r // blk) != (q // blk)))
        blk *= 2
    masks = np.stack(masks).astype(np.float32)
    shift = np.zeros(((CONV_W - 1) * c, CARRY_ROWS + c), np.float32)
    for k in range(CONV_W - 1):
        shift[k * c + np.arange(c), CARRY_ROWS + np.arange(c) - (CONV_W - 1) + k] = 1.0
    return jnp.asarray(masks), jnp.asarray(shift, dtype=BF16)


def _tri_inv(a_list, masks_ref, c):
    pw = [a * masks_ref[M_DIAG] for a in a_list]
    p = [masks_ref[M_EYE] - d for d in pw]
    for _ in range(int(math.log2(TRI_BLOCK)) - 1):
        pw = [_bdot(x, x) for x in pw]
        p = [x + _bdot(x, y) for x, y in zip(p, pw)]
    blk, level = TRI_BLOCK, M_MERGE0
    while blk < c:
        t = [_bdot(x, a * masks_ref[level]) for x, a in zip(p, a_list)]
        p = [x - _bdot(y, x) for x, y in zip(p, t)]
        blk, level = 2 * blk, level + 1
    return p


def _gdn_kernel(qkv_ref, z_ref, ba_ref, convw_ref, alog_ref, dtb_ref, gnw_ref, conv0_ref, s0_ref,
                masks_ref, shift_ref, o_ref, s_ref, xbuf_ref, *, c, bb):
    @pl.when(pl.program_id(1) == 0)
    def _():
        xbuf_ref[:, 0:CARRY_ROWS, :] = conv0_ref[...]
        s_ref[...] = s0_ref[...]

    heads = range(GDN_HEADS)
    col = lambda arr, h: arr[:, h:h + 1]
    lanes = lambda base, h: slice(base + h * GDN_DK, base + (h + 1) * GDN_DK)
    rows = lambda arr, i: arr[i * c:(i + 1) * c, :]

    def conv_silu(i):
        w = convw_ref[...]
        xbuf_ref[i, CARRY_ROWS:CARRY_ROWS + c, :] = qkv_ref[i]
        shifted = jnp.dot(shift_ref[...], xbuf_ref[i], preferred_element_type=F32)
        y = shifted[0:c, :] * w[0:1, :]
        y = y + shifted[c:2 * c, :] * w[1:2, :]
        y = y + shifted[2 * c:3 * c, :] * w[2:3, :]
        y = y + qkv_ref[i].astype(F32) * w[3:4, :]
        xbuf_ref[i, 0:CARRY_ROWS, :] = xbuf_ref[i, c:c + CARRY_ROWS, :]
        return _silu(y)

    def per_sequence(i):
        y = conv_silu(i)

        beta = _sigmoid(ba_ref[i, :, 0:LANES])
        sp_in = ba_ref[i, :, LANES:2 * LANES] + dtb_ref[...]
        softplus = jnp.maximum(sp_in, 0.0) + jnp.log(1.0 + jnp.exp(-jnp.abs(sp_in)))
        g = -jnp.exp(alog_ref[...]) * softplus
        gc = jnp.dot(masks_ref[M_CAUSAL], g, precision=lax.Precision.HIGHEST,
                     preferred_element_type=F32)
        gc_t = jnp.concatenate([gc, jnp.zeros((LANES - c, LANES), F32)], axis=0).T
        g_last = gc[c - 1:c, :]
        e_gc = jnp.exp(gc)

        def inv_norm(base, fill):
            sq = jnp.concatenate([jnp.square(y[:, lanes(base, h)]) for h in heads], axis=0)
            tot = _bdot(sq, jnp.full((GDN_DK, LANES), fill, F32))
            return lax.rsqrt(tot + fill * L2_EPS)

        return dict(y=y, beta=beta, gc=gc, gc_t=gc_t, e_gc=e_gc, e_kd=jnp.exp(g_last - gc),
                    e_last=jnp.exp(g_last), beta_e=beta * e_gc,
                    inv_q=inv_norm(0, float(GDN_DK)), inv_k=inv_norm(GDN_QK, 1.0))

    seqs = [per_sequence(i) for i in range(bb)]
    probs = [(i, h) for i in range(bb) for h in heads]
    qn = [seqs[i]["y"][:, lanes(0, h)] * rows(seqs[i]["inv_q"], h) for i, h in probs]
    kn = [seqs[i]["y"][:, lanes(GDN_QK, h)] * rows(seqs[i]["inv_k"], h) for i, h in probs]
    kb = [kn[j] * col(seqs[i]["beta"], h) for j, (i, h) in enumerate(probs)]
    qk = [_bdot_nt(jnp.concatenate([qn[j], kb[j]], axis=0), kn[j]) for j in range(len(probs))]
    attn, a_mat = [], []
    for j, (i, h) in enumerate(probs):
        diff = col(seqs[i]["gc"], h) - seqs[i]["gc_t"][h:h + 1, 0:c]
        decay = jnp.exp(jnp.minimum(diff, 0.0))
        attn.append(qk[j][0:c, :] * (decay * masks_ref[M_CAUSAL]))
        a_mat.append(qk[j][c:2 * c, :] * (decay * masks_ref[M_STRICT]))
    t_inv = _tri_inv(a_mat, masks_ref, c)
    w_u = [_bdot(t_inv[j], jnp.concatenate([seqs[i]["y"][:, lanes(2 * GDN_QK, h)] * col(seqs[i]["beta"], h),
                                            kn[j] * col(seqs[i]["beta_e"], h)], axis=1))
           for j, (i, h) in enumerate(probs)]
    x_s = [_bdot(jnp.concatenate([w_u[j][:, GDN_DV:2 * GDN_DV], qn[j] * col(seqs[i]["e_gc"], h)], axis=0),
                 s_ref[i, h]) for j, (i, h) in enumerate(probs)]
    u = [w_u[j][:, 0:GDN_DV] - x_s[j][0:c, :] for j in range(len(probs))]
    o = [x_s[j][c:2 * c, :] + _bdot(attn[j], u[j]) for j in range(len(probs))]
    ds = [_bdot_tn(kn[j] * col(seqs[i]["e_kd"], h), u[j]) for j, (i, h) in enumerate(probs)]
    gnw = gnw_ref[...]
    for j, (i, h) in enumerate(probs):
        s_ref[i, h] = s_ref[i, h] * col(seqs[i]["e_last"], h) + ds[j]
        out = o[j] * _rms_scale(o[j]) * gnw
        out = out * _silu(z_ref[i, :, lanes(0, h)].astype(F32))
        o_ref[i, :, lanes(0, h)] = out.astype(BF16)


def _gdn_mixer(p3, ba3, conv_w, alog_row, dtb_row, gnw_row, conv0, s0):
    batch, seq, _ = p3.shape
    c = min(CHUNK, seq)
    bb = min(GDN_SEQS_PER_STEP, batch)
    masks, shift = _gdn_constants(c)
    const = lambda shape: pl.BlockSpec(shape, lambda b, s: (0,) * len(shape))
    state_spec = pl.BlockSpec((bb, GDN_HEADS, GDN_DK, GDN_DV), lambda b, s: (b, 0, 0, 0))
    return pl.pallas_call(
        functools.partial(_gdn_kernel, c=c, bb=bb),
        grid=(batch // bb, seq // c),
        in_specs=[
            pl.BlockSpec((bb, c, GDN_CONV_DIM), lambda b, s: (b, s, 0)),
            pl.BlockSpec((bb, c, GDN_V), lambda b, s: (b, s, GDN_CONV_DIM // GDN_V)),
            pl.BlockSpec((bb, c, BA_WIDTH), lambda b, s: (b, s, 0)),
            const((CONV_W, GDN_CONV_DIM)),
            const((1, LANES)),
            const((1, LANES)),
            const((1, GDN_DV)),
            pl.BlockSpec((bb, CARRY_ROWS, GDN_CONV_DIM), lambda b, s: (b, 0, 0)),
            state_spec,
            const(masks.shape),
            const(shift.shape),
        ],
        out_specs=[
            pl.BlockSpec((bb, c, GDN_V), lambda b, s: (b, s, 0)),
            state_spec,
        ],
        out_shape=[
            jax.ShapeDtypeStruct((batch, seq, GDN_V), BF16),
            jax.ShapeDtypeStruct((batch, GDN_HEADS, GDN_DK, GDN_DV), F32),
        ],
        scratch_shapes=[pltpu.VMEM((bb, CARRY_ROWS + c, GDN_CONV_DIM), BF16)],
        compiler_params=_params(2),
        name="gdn_mixer",
    )(p3, p3, ba3, conv_w, alog_row, dtb_row, gnw_row, conv0, s0, masks, shift)


def _rope_kernel(inv_ref, cos_ref, sin_ref, *, rows, offset):
    pos = lax.broadcasted_iota(jnp.int32, (rows, RET_DK), 0) + (pl.program_id(0) * rows + offset)
    lane = lax.broadcasted_iota(jnp.int32, (rows, RET_DK), 1)
    ang = pos.astype(F32) * inv_ref[...]
    cos_ref[...] = jnp.cos(ang)
    sin_ref[...] = jnp.where(lane % 2 == 0, -1.0, 1.0) * jnp.sin(ang)


def _rope_tables(inv_row, seq, offset):
    rows = min(512, seq)
    return pl.pallas_call(
        functools.partial(_rope_kernel, rows=rows, offset=offset),
        grid=(seq // rows,),
        in_specs=[pl.BlockSpec((1, RET_DK), lambda i: (0, 0))],
        out_specs=[pl.BlockSpec((rows, RET_DK), lambda i: (i, 0))] * 2,
        out_shape=[jax.ShapeDtypeStruct((seq, RET_DK), F32)] * 2,
        compiler_params=_params(1),
        name="rope_tables",
    )(inv_row)


RET_CHUNK = 256
RET_SEQS_PER_STEP = 2


def _ret_kernel(q_ref, k_ref, v_ref, g_ref, cos_ref, sin_ref, gnw_ref, s0_ref, swap_ref, o_ref, s_ref,
                dmat_ref, scale_ref, *, c, bb):
    first_block = pl.program_id(1) == 0

    @pl.when(first_block)
    def _():
        s_ref[...] = s0_ref[...]

    @pl.when(jnp.logical_and(pl.program_id(0) == 0, first_block))
    def _():
        r = lax.broadcasted_iota(jnp.int32, (c, c), 0)
        q_i = lax.broadcasted_iota(jnp.int32, (c, c), 1)
        causal = r >= q_i
        dpos = (r - q_i).astype(F32)
        tok = lax.broadcasted_iota(jnp.int32, (c, RET_DK), 0).astype(F32)
        for h in range(RET_HEADS):
            lg = math.log(1.0 - 2.0 ** (-5.0 - h))
            dmat = jnp.where(causal, jnp.exp(jnp.where(causal, dpos * lg, 0.0)), 0.0)
            dmat_ref[h] = dmat * (RET_DK ** -0.5)
            scale_ref[0, h] = jnp.exp((tok + 1.0) * lg)
            scale_ref[1, h] = jnp.exp((c - 1.0 - tok) * lg) * (RET_DK ** -0.5)

    cos = cos_ref[...]
    sin = sin_ref[...]
    swap = swap_ref[...]

    def rotate(x):
        partner = jnp.dot(x, swap, preferred_element_type=F32)
        return x.astype(F32) * cos + partner * sin

    lanes = lambda h: slice(h * RET_DK, (h + 1) * RET_DK)
    probs = [(i, h) for i in range(bb) for h in range(RET_HEADS)]
    qr = [rotate(q_ref[i, :, lanes(h)]) for i, h in probs]
    kr = [rotate(k_ref[i, :, lanes(h)]) for i, h in probs]
    attn = [_bdot_nt(qr[j], kr[j]) * dmat_ref[h] for j, (i, h) in enumerate(probs)]
    o = [_bdot(qr[j] * scale_ref[0, h], s_ref[i, h]) + _bdot(attn[j], v_ref[i, :, lanes(h)])
         for j, (i, h) in enumerate(probs)]
    ds = [_bdot_tn(kr[j] * scale_ref[1, h], v_ref[i, :, lanes(h)]) for j, (i, h) in enumerate(probs)]
    for j, (i, h) in enumerate(probs):
        s_ref[i, h] = s_ref[i, h] * math.exp(c * math.log(1.0 - 2.0 ** (-5.0 - h))) + ds[j]
        mu = jnp.mean(o[j], axis=-1, keepdims=True)
        dev = o[j] - mu
        var = jnp.mean(jnp.square(dev), axis=-1, keepdims=True)
        out = dev * lax.rsqrt(var + GN_EPS) * gnw_ref[:, lanes(h)]
        out = out * _silu(g_ref[i, :, lanes(h)].astype(F32))
        o_ref[i, :, lanes(h)] = out.astype(BF16)


def _ret_mixer(p3, cos, sin, gnw_row, s0):
    batch, seq, _ = p3.shape
    c = min(RET_CHUNK, seq)
    bb = min(RET_SEQS_PER_STEP, batch)
    pair = np.arange(RET_DK) ^ 1
    swap = jnp.asarray(np.eye(RET_DK, dtype=np.float32)[:, pair], dtype=BF16)
    col = lambda j: pl.BlockSpec((bb, c, RET_QK), lambda b, s: (b, s, j))
    state_spec = pl.BlockSpec((bb, RET_HEADS, RET_DK, RET_DV), lambda b, s: (b, 0, 0, 0))
    return pl.pallas_call(
        functools.partial(_ret_kernel, c=c, bb=bb),
        grid=(batch // bb, seq // c),
        in_specs=[
            col(4), col(5), col(6), col(7),
            pl.BlockSpec((c, RET_DK), lambda b, s: (s, 0)),
            pl.BlockSpec((c, RET_DK), lambda b, s: (s, 0)),
            pl.BlockSpec((1, RET_V), lambda b, s: (0, 0)),
            state_spec,
            pl.BlockSpec((RET_DK, RET_DK), lambda b, s: (0, 0)),
        ],
        out_specs=[
            pl.BlockSpec((bb, c, RET_V), lambda b, s: (b, s, 0)),
            state_spec,
        ],
        out_shape=[
            jax.ShapeDtypeStruct((batch, seq, RET_V), BF16),
            jax.ShapeDtypeStruct((batch, RET_HEADS, RET_DK, RET_DV), F32),
        ],
        scratch_shapes=[pltpu.VMEM((RET_HEADS, c, c), F32), pltpu.VMEM((2, RET_HEADS, c, RET_DK), F32)],
        compiler_params=_params(2),
        name="ret_mixer",
    )(p3, p3, p3, p3, cos, sin, gnw_row, s0, swap)


def _out_proj_kernel(x_ref, ma_ref, mr_ref, woa_ref, wor_ref, nw_ref, x1_ref, h2_ref):
    x1 = x_ref[...] + jnp.dot(ma_ref[...], woa_ref[...], preferred_element_type=F32)
    x1 = x1 + jnp.dot(mr_ref[...], wor_ref[...], preferred_element_type=F32)
    x1_ref[...] = x1
    h2_ref[...] = (x1 * _rms_scale(x1) * nw_ref[...]).astype(BF16)


def _out_proj(x2d, mix_a, mix_r, w_out_a, w_out_r, norm_w):
    t, d = x2d.shape
    tm = min(OUT_PROJ_TM, t)
    return pl.pallas_call(
        _out_proj_kernel,
        grid=(t // tm,),
        in_specs=[
            pl.BlockSpec((tm, d), lambda i: (i, 0)),
            pl.BlockSpec((tm, GDN_V), lambda i: (i, 0)),
            pl.BlockSpec((tm, RET_V), lambda i: (i, 0)),
            pl.BlockSpec((GDN_V, d), lambda i: (0, 0)),
            pl.BlockSpec((RET_V, d), lambda i: (0, 0)),
            pl.BlockSpec((1, d), lambda i: (0, 0)),
        ],
        out_specs=[
            pl.BlockSpec((tm, d), lambda i: (i, 0)),
            pl.BlockSpec((tm, d), lambda i: (i, 0)),
        ],
        out_shape=[
            jax.ShapeDtypeStruct((t, d), F32),
            jax.ShapeDtypeStruct((t, d), BF16),
        ],
        compiler_params=_params(1),
        name="out_proj",
    )(x2d, mix_a, mix_r, w_out_a, w_out_r, norm_w)


def _ffn_kernel(h2_ref, x1_ref, wg_ref, wu_ref, wd_ref, fw_ref, y_ref, *, n_ff, final_norm):
    f = pl.program_id(1)

    @pl.when(f == 0)
    def _():
        y_ref[...] = x1_ref[...]

    h = h2_ref[...]
    gate = jnp.dot(h, wg_ref[...], preferred_element_type=F32)
    up = jnp.dot(h, wu_ref[...], preferred_element_type=F32)
    act = (_silu(gate) * up).astype(BF16)
    y_ref[...] += jnp.dot(act, wd_ref[...], preferred_element_type=F32)

    if final_norm:
        @pl.when(f == n_ff - 1)
        def _():
            x2 = y_ref[...]
            y_ref[...] = x2 * _rms_scale(x2) * fw_ref[...]


def _ffn(h2, x1, w_gate_up, w_down, final_w, final_norm):
    t, d = x1.shape
    d_ff = w_down.shape[0]
    tm = min(FFN_TM, t)
    tf = FFN_TF
    n_ff = d_ff // tf
    return pl.pallas_call(
        functools.partial(_ffn_kernel, n_ff=n_ff, final_norm=final_norm),
        grid=(t // tm, n_ff),
        in_specs=[
            pl.BlockSpec((tm, d), lambda i, f: (i, 0)),
            pl.BlockSpec((tm, d), lambda i, f: (i, 0)),
            pl.BlockSpec((d, tf), lambda i, f: (0, f)),
            pl.BlockSpec((d, tf), lambda i, f: (0, n_ff + f)),
            pl.BlockSpec((tf, d), lambda i, f: (f, 0)),
            pl.BlockSpec((1, d), lambda i, f: (0, 0)),
        ],
        out_specs=pl.BlockSpec((tm, d), lambda i, f: (i, 0)),
        out_shape=jax.ShapeDtypeStruct((t, d), F32),
        compiler_params=_params(2, VMEM_LIMIT_BIG_TILE_BYTES),
        name="ffn",
    )(h2, x1, w_gate_up, w_gate_up, w_down, final_w)


def _prep_layer_weights(w_in, w_out, w_gate_up, w_down, a_log, dt_bias, gdn_norm_w, ret_gn_w,
                        attn_norm_w, ffn_norm_w, conv_w):
    d = w_in.shape[0]
    o_b = GDN_CONV_DIM + GDN_V
    o_qr = o_b + 2 * GDN_HEADS
    w_main = jnp.concatenate([w_in[:, 0:o_b], w_in[:, o_qr:]], axis=1).astype(BF16)
    pad = jnp.zeros((d, LANES - GDN_HEADS), w_in.dtype)
    w_ba = jnp.concatenate([w_in[:, o_b:o_b + GDN_HEADS], pad,
                            w_in[:, o_b + GDN_HEADS:o_qr], pad], axis=1).astype(BF16)
    row_pad = lambda v: jnp.pad(v.astype(F32), (0, LANES - v.shape[0])).reshape(1, LANES)
    return dict(
        w_main=w_main, w_ba=w_ba,
        w_out_a=w_out[0:GDN_V].astype(BF16), w_out_r=w_out[GDN_V:].astype(BF16),
        w_gate_up=w_gate_up.astype(BF16), w_down=w_down.astype(BF16),
        alog_row=row_pad(a_log), dtb_row=row_pad(dt_bias),
        gnw_row=gdn_norm_w.reshape(1, GDN_DV).astype(F32),
        ret_gnw_row=ret_gn_w.reshape(1, RET_V).astype(F32),
        attn_norm_w=attn_norm_w.reshape(1, d), ffn_norm_w=ffn_norm_w.reshape(1, d),
        conv_w=conv_w,
    )


def _layer(x2d, batch, seq, cos, sin, conv_state, s_gdn, s_ret, lw, final_w, final_norm):
    p, ba = _in_proj(x2d, lw["attn_norm_w"], lw["w_main"], lw["w_ba"])
    p3 = p.reshape(batch, seq, P_WIDTH)
    conv0 = jnp.pad(conv_state.astype(BF16), ((0, 0), (CARRY_ROWS - (CONV_W - 1), 0), (0, 0)))
    mix_a, s_gdn_new = _gdn_mixer(p3, ba.reshape(batch, seq, BA_WIDTH), lw["conv_w"], lw["alog_row"],
                                  lw["dtb_row"], lw["gnw_row"], conv0, s_gdn.astype(F32))
    mix_r, s_ret_new = _ret_mixer(p3, cos, sin, lw["ret_gnw_row"], s_ret.astype(F32))
    mix_a = mix_a.reshape(batch * seq, GDN_V)
    mix_r = mix_r.reshape(batch * seq, RET_V)
    x1, h2 = _out_proj(x2d, mix_a, mix_r, lw["w_out_a"], lw["w_out_r"], lw["ffn_norm_w"])
    y = _ffn(h2, x1, lw["w_gate_up"], lw["w_down"], final_w, final_norm)
    rows = p3[:, seq - (CONV_W - 1):, 0:GDN_CONV_DIM].astype(conv_state.dtype)
    return y, rows, s_gdn_new.astype(s_gdn.dtype), s_ret_new.astype(s_ret.dtype)


def kernel(x_prompt, x_sample, state_gdn_conv, state_gdn, state_ret, attn_norm_w, w_in, conv_w, a_log, dt_bias, gdn_norm_w, ret_gn_w, w_out, ffn_norm_w, w_gate_up, w_down, final_norm_w):
    depth = w_in.shape[0]
    bp, lp, d = x_prompt.shape
    bs, ls, _ = x_sample.shape
    assert lp >= CONV_W - 1 and ls >= CONV_W - 1
    inv = 1.0 / (ROPE_BASE ** jnp.linspace(0.0, 1.0, RET_DK // 2, dtype=F32))
    inv_row = jnp.repeat(inv, 2).reshape(1, RET_DK)
    cos_p, sin_p = _rope_tables(inv_row, lp, 0)
    cos_s, sin_s = _rope_tables(inv_row, ls, PAST_LEN)
    final_w = final_norm_w.reshape(1, d)
    hp = x_prompt.reshape(bp * lp, d)
    hs = x_sample.reshape(bs * ls, d)
    outs = [[] for _ in range(6)]
    for l in range(depth):
        lw = _prep_layer_weights(w_in[l], w_out[l], w_gate_up[l], w_down[l], a_log[l], dt_bias[l],
                                 gdn_norm_w[l], ret_gn_w[l], attn_norm_w[l], ffn_norm_w[l], conv_w[l])
        last = l == depth - 1
        conv0 = jnp.zeros((bp, CONV_W - 1, GDN_CONV_DIM), x_prompt.dtype)
        sg0 = jnp.zeros((bp, GDN_HEADS, GDN_DK, GDN_DV), state_gdn.dtype)
        sr0 = jnp.zeros((bp, RET_HEADS, RET_DK, RET_DV), state_ret.dtype)
        hp, c_p, g_p, r_p = _layer(hp, bp, lp, cos_p, sin_p, conv0, sg0, sr0, lw, final_w, last)
        hs, c_s, g_s, r_s = _layer(hs, bs, ls, cos_s, sin_s, state_gdn_conv[l], state_gdn[l],
                                   state_ret[l], lw, final_w, last)
        for acc, val in zip(outs, (c_p, g_p, r_p, c_s, g_s, r_s)):
            acc.append(val)
    return (hp.reshape(bp, lp, d), hs.reshape(bs, ls, d)) + tuple(jnp.stack(o) for o in outs)
```

```python
import functools
import math

import jax
import jax.numpy as jnp
import numpy as np
from jax import lax
from jax.experimental import pallas as pl
from jax.experimental.pallas import tpu as pltpu

F32 = jnp.float32
BF16 = jnp.bfloat16

GDN_HEADS = 8
GDN_DK = 128
GDN_DV = 128
CONV_W = 4
RET_HEADS = 4
RET_DK = 256
RET_DV = 256
GDN_QK = GDN_HEADS * GDN_DK
GDN_V = GDN_HEADS * GDN_DV
GDN_CONV_DIM = 2 * GDN_QK + GDN_V
RET_QK = RET_HEADS * RET_DK
RET_V = RET_HEADS * RET_DV
CHUNK = 64
PAST_LEN = 4096
RMS_EPS = 1e-6
GN_EPS = 1e-5
L2_EPS = 1e-6
ROPE_BASE = 10000.0

LANES = 128
SUBLANES = 8
V7X_VMEM_BYTES = 64 * 1024 * 1024
VMEM_LIMIT_BYTES = 52 * 1024 * 1024
VMEM_LIMIT_BIG_TILE_BYTES = 60 * 1024 * 1024
TRI_BLOCK = 16

P_WIDTH = 8 * 1024
BA_WIDTH = 2 * LANES

IN_PROJ_TM, IN_PROJ_TN = 1024, 2048
OUT_PROJ_TM = 512
FFN_TM, FFN_TF = 1024, 512


def _bdot(a, b):
    return jnp.dot(a.astype(BF16), b.astype(BF16), preferred_element_type=F32)


def _bdot_nt(a, b):
    return lax.dot_general(a.astype(BF16), b.astype(BF16), (((1,), (1,)), ((), ())),
                           preferred_element_type=F32)


def _bdot_tn(a, b):
    return lax.dot_general(a.astype(BF16), b.astype(BF16), (((0,), (0,)), ((), ())),
                           preferred_element_type=F32)


def _select_rows_exact(sel, x):
    hi = x.astype(BF16)
    rest = x - hi.astype(F32)
    mid = rest.astype(BF16)
    lo = (rest - mid.astype(F32)).astype(BF16)
    n = x.shape[1]
    parts = jnp.dot(sel, jnp.concatenate([hi, mid, lo], axis=1), preferred_element_type=F32)
    return parts[:, 0:n] + parts[:, n:2 * n] + parts[:, 2 * n:3 * n]


def _sigmoid(x):
    return 1.0 / (1.0 + jnp.exp(-x))


def _silu(x):
    return x * _sigmoid(x)


def _rms_scale(x):
    return lax.rsqrt(jnp.mean(x * x, axis=-1, keepdims=True) + RMS_EPS)


def _params(n_grid_dims, vmem_limit_bytes=VMEM_LIMIT_BYTES):
    return pltpu.CompilerParams(dimension_semantics=("arbitrary",) * n_grid_dims,
                                vmem_limit_bytes=vmem_limit_bytes)


def _in_proj_kernel(x_ref, nw_ref, w_ref, wba_ref, p_ref, ba_ref, h_ref):
    @pl.when(pl.program_id(1) == 0)
    def _():
        x = x_ref[...]
        h = (x * _rms_scale(x) * nw_ref[...]).astype(BF16)
        h_ref[...] = h
        ba_ref[...] = jnp.dot(h, wba_ref[...], preferred_element_type=F32)

    p_ref[...] = jnp.dot(h_ref[...], w_ref[...], preferred_element_type=F32).astype(BF16)


def _in_proj(x2d, norm_w, w_main, w_ba):
    t, d = x2d.shape
    tm = min(IN_PROJ_TM, t)
    tn = IN_PROJ_TN
    return pl.pallas_call(
        _in_proj_kernel,
        grid=(t // tm, P_WIDTH // tn),
        in_specs=[
            pl.BlockSpec((tm, d), lambda i, j: (i, 0)),
            pl.BlockSpec((1, d), lambda i, j: (0, 0)),
            pl.BlockSpec((d, tn), lambda i, j: (0, j)),
            pl.BlockSpec((d, BA_WIDTH), lambda i, j: (0, 0)),
        ],
        out_specs=[
            pl.BlockSpec((tm, tn), lambda i, j: (i, j)),
            pl.BlockSpec((tm, BA_WIDTH), lambda i, j: (i, 0)),
        ],
        out_shape=[
            jax.ShapeDtypeStruct((t, P_WIDTH), BF16),
            jax.ShapeDtypeStruct((t, BA_WIDTH), F32),
        ],
        scratch_shapes=[pltpu.VMEM((tm, d), BF16)],
        compiler_params=_params(2, VMEM_LIMIT_BIG_TILE_BYTES),
        name="in_proj",
    )(x2d, norm_w, w_main, w_ba)


M_CAUSAL, M_STRICT, M_EYE, M_DIAG, M_MERGE0 = 0, 1, 2, 3, 4
CARRY_ROWS = 16
GDN_SEQS_PER_STEP = 4
GDN_CHUNKS_PER_STEP = 1


def _gdn_constants(c):
    r = np.arange(c)[:, None]
    q = np.arange(c)[None, :]
    masks = [r >= q, r > q, r == q, (r // TRI_BLOCK) == (q // TRI_BLOCK)]
    blk = TRI_BLOCK
    while blk < c:
        masks.append(((r // (2 * blk)) == (q // (2 * blk))) & ((r // blk) != (q // blk)))
        blk *= 2
    masks = np.stack(masks).astype(np.float32)
    shift = np.zeros(((CONV_W - 1) * c, CARRY_ROWS + c), np.float32)
    for k in range(CONV_W - 1):
        shift[k * c + np.arange(c), CARRY_ROWS + np.arange(c) - (CONV_W - 1) + k] = 1.0
    return jnp.asarray(masks), jnp.asarray(shift, dtype=BF16)


def _tri_inv(a_list, masks_ref, c):
    pw = [a * masks_ref[M_DIAG] for a in a_list]
    p = [masks_ref[M_EYE] - d for d in pw]
    for _ in range(int(math.log2(TRI_BLOCK)) - 1):
        pw = [_bdot(x, x) for x in pw]
        p = [x + _bdot(x, y) for x, y in zip(p, pw)]
    blk, level = TRI_BLOCK, M_MERGE0
    while blk < c:
        t = [_bdot(x, a * masks_ref[level]) for x, a in zip(p, a_list)]
        p = [x - _bdot(y, x) for x, y in zip(p, t)]
        blk, level = 2 * blk, level + 1
    return p


def _gdn_kernel(qkv_ref, z_ref, ba_ref, convw_ref, alog_ref, dtb_ref, gnw_ref, conv0_ref, s0_ref,
                masks_ref, shift_ref, o_ref, s_ref, xbuf_ref, *, c, bb, n_sub):
    @pl.when(pl.program_id(1) == 0)
    def _():
        xbuf_ref[:, 0:CARRY_ROWS, :] = conv0_ref[...]
        s_ref[...] = s0_ref[...]

    heads = range(GDN_HEADS)
    col = lambda arr, h: arr[:, h:h + 1]
    lanes = lambda base, h: slice(base + h * GDN_DK, base + (h + 1) * GDN_DK)
    rows = lambda arr, i: arr[i * c:(i + 1) * c, :]
    toks = lambda sub: slice(sub * c, (sub + 1) * c)

    def conv_silu(i, sub):
        w = convw_ref[...]
        xbuf_ref[i, CARRY_ROWS:CARRY_ROWS + c, :] = qkv_ref[i, toks(sub), :]
        shifted = jnp.dot(shift_ref[...], xbuf_ref[i], preferred_element_type=F32)
        y = shifted[0:c, :] * w[0:1, :]
        y = y + shifted[c:2 * c, :] * w[1:2, :]
        y = y + shifted[2 * c:3 * c, :] * w[2:3, :]
        y = y + qkv_ref[i, toks(sub), :].astype(F32) * w[3:4, :]
        xbuf_ref[i, 0:CARRY_ROWS, :] = xbuf_ref[i, c:c + CARRY_ROWS, :]
        return _silu(y)

    cumsum_tokens = lambda g: _select_rows_exact(masks_ref[M_CAUSAL].astype(BF16), g)

    def per_chunk(i, sub):
        y = conv_silu(i, sub)

        beta = _sigmoid(ba_ref[i, toks(sub), 0:LANES])
        sp_in = ba_ref[i, toks(sub), LANES:2 * LANES] + dtb_ref[...]
        softplus = jnp.maximum(sp_in, 0.0) + jnp.log(1.0 + jnp.exp(-jnp.abs(sp_in)))
        gc = cumsum_tokens(-jnp.exp(alog_ref[...]) * softplus)
        gc_t = jnp.concatenate([gc, jnp.zeros((LANES - c, LANES), F32)], axis=0).T
        g_last = gc[c - 1:c, :]
        e_gc = jnp.exp(gc)

        def inv_norm(base, fill):
            sq = jnp.concatenate([jnp.square(y[:, lanes(base, h)]) for h in heads], axis=0)
            tot = _bdot(sq, jnp.full((GDN_DK, LANES), fill, F32))
            return lax.rsqrt(tot + fill * L2_EPS)

        return dict(y=y, beta=beta, gc=gc, gc_t=gc_t, e_gc=e_gc, e_kd=jnp.exp(g_last - gc),
                    e_last=jnp.exp(g_last), beta_e=beta * e_gc,
                    inv_q=inv_norm(0, float(GDN_DK)), inv_k=inv_norm(GDN_QK, 1.0))

    units = [(sub, i) for sub in range(n_sub) for i in range(bb)]
    pre = {unit: per_chunk(unit[1], unit[0]) for unit in units}
    probs = [(unit, h) for unit in units for h in heads]
    qn = [pre[u]["y"][:, lanes(0, h)] * rows(pre[u]["inv_q"], h) for u, h in probs]
    kn = [pre[u]["y"][:, lanes(GDN_QK, h)] * rows(pre[u]["inv_k"], h) for u, h in probs]
    kb = [kn[j] * col(pre[u]["beta"], h) for j, (u, h) in enumerate(probs)]
    qk = [_bdot_nt(jnp.concatenate([qn[j], kb[j]], axis=0), kn[j]) for j in range(len(probs))]
    attn, a_mat = [], []
    for j, (u, h) in enumerate(probs):
        diff = col(pre[u]["gc"], h) - pre[u]["gc_t"][h:h + 1, 0:c]
        decay = jnp.exp(jnp.minimum(diff, 0.0))
        attn.append(qk[j][0:c, :] * (decay * masks_ref[M_CAUSAL]))
        a_mat.append(qk[j][c:2 * c, :] * (decay * masks_ref[M_STRICT]))
    t_inv = _tri_inv(a_mat, masks_ref, c)
    w_u = [_bdot(t_inv[j], jnp.concatenate([pre[u]["y"][:, lanes(2 * GDN_QK, h)] * col(pre[u]["beta"], h),
                                            kn[j] * col(pre[u]["beta_e"], h)], axis=1))
           for j, (u, h) in enumerate(probs)]
    gnw = gnw_ref[...]
    for sub in range(n_sub):
        mine = [(j, u[1], h) for j, (u, h) in enumerate(probs) if u[0] == sub]
        x_s = {j: _bdot(jnp.concatenate([w_u[j][:, GDN_DV:2 * GDN_DV],
                                         qn[j] * col(pre[(sub, i)]["e_gc"], h)], axis=0), s_ref[i, h])
               for j, i, h in mine}
        u_new = {j: w_u[j][:, 0:GDN_DV] - x_s[j][0:c, :] for j, i, h in mine}
        o = {j: x_s[j][c:2 * c, :] + _bdot(attn[j], u_new[j]) for j, i, h in mine}
        ds = {j: _bdot_tn(kn[j] * col(pre[(sub, i)]["e_kd"], h), u_new[j]) for j, i, h in mine}
        for j, i, h in mine:
            s_ref[i, h] = s_ref[i, h] * col(pre[(sub, i)]["e_last"], h) + ds[j]
            out = o[j] * _rms_scale(o[j]) * gnw
            out = out * _silu(z_ref[i, toks(sub), lanes(0, h)].astype(F32))
            o_ref[i, toks(sub), lanes(0, h)] = out.astype(BF16)


def _gdn_mixer(p3, ba3, conv_w, alog_row, dtb_row, gnw_row, conv0, s0):
    batch, seq, _ = p3.shape
    c = min(CHUNK, seq)
    bb = min(GDN_SEQS_PER_STEP, batch)
    n_sub = min(GDN_CHUNKS_PER_STEP, seq // c)
    rows = n_sub * c
    masks, shift = _gdn_constants(c)
    const = lambda shape: pl.BlockSpec(shape, lambda b, s: (0,) * len(shape))
    state_spec = pl.BlockSpec((bb, GDN_HEADS, GDN_DK, GDN_DV), lambda b, s: (b, 0, 0, 0))
    return pl.pallas_call(
        functools.partial(_gdn_kernel, c=c, bb=bb, n_sub=n_sub),
        grid=(batch // bb, seq // rows),
        in_specs=[
            pl.BlockSpec((bb, rows, GDN_CONV_DIM), lambda b, s: (b, s, 0)),
            pl.BlockSpec((bb, rows, GDN_V), lambda b, s: (b, s, GDN_CONV_DIM // GDN_V)),
            pl.BlockSpec((bb, rows, BA_WIDTH), lambda b, s: (b, s, 0)),
            const((CONV_W, GDN_CONV_DIM)),
            const((1, LANES)),
            const((1, LANES)),
            const((1, GDN_DV)),
            pl.BlockSpec((bb, CARRY_ROWS, GDN_CONV_DIM), lambda b, s: (b, 0, 0)),
            state_spec,
            const(masks.shape),
            const(shift.shape),
        ],
        out_specs=[
            pl.BlockSpec((bb, rows, GDN_V), lambda b, s: (b, s, 0)),
            state_spec,
        ],
        out_shape=[
            jax.ShapeDtypeStruct((batch, seq, GDN_V), BF16),
            jax.ShapeDtypeStruct((batch, GDN_HEADS, GDN_DK, GDN_DV), F32),
        ],
        scratch_shapes=[pltpu.VMEM((bb, CARRY_ROWS + c, GDN_CONV_DIM), BF16)],
        compiler_params=_params(2),
        name="gdn_mixer",
    )(p3, p3, ba3, conv_w, alog_row, dtb_row, gnw_row, conv0, s0, masks, shift)


def _rope_kernel(inv_ref, cos_ref, sin_ref, *, rows, offset):
    pos = lax.broadcasted_iota(jnp.int32, (rows, LANES), 0) + (pl.program_id(0) * rows + offset)
    ang = pos.astype(F32) * inv_ref[...]
    cos_ref[...] = jnp.cos(ang)
    sin_ref[...] = jnp.sin(ang)


def _rope_tables(inv_row, seq, offset):
    rows = min(512, seq)
    return pl.pallas_call(
        functools.partial(_rope_kernel, rows=rows, offset=offset),
        grid=(seq // rows,),
        in_specs=[pl.BlockSpec((1, LANES), lambda i: (0, 0))],
        out_specs=[pl.BlockSpec((rows, LANES), lambda i: (i, 0))] * 2,
        out_shape=[jax.ShapeDtypeStruct((seq, LANES), F32)] * 2,
        compiler_params=_params(1),
        name="rope_tables",
    )(inv_row)


RET_CHUNK = 256
RET_SEQS_PER_STEP = 2


def _pair_split_matrix():
    perm = np.concatenate([np.arange(0, RET_DK, 2), np.arange(1, RET_DK, 2)])
    return np.eye(RET_DK, dtype=np.float32)[:, perm]


def _ret_kernel(q_ref, k_ref, v_ref, g_ref, cos_ref, sin_ref, gnw_ref, s0_ref, split_ref, split_t_ref,
                o_ref, s_ref, dmat_ref, scale_ref, *, c, bb, n_blocks):
    first_block = pl.program_id(1) == 0
    probs = [(i, h) for i in range(bb) for h in range(RET_HEADS)]

    @pl.when(first_block)
    def _():
        for i, h in probs:
            s_ref[i, h] = _select_rows_exact(split_t_ref[...], s0_ref[i, h])

    @pl.when(jnp.logical_and(pl.program_id(0) == 0, first_block))
    def _():
        r = lax.broadcasted_iota(jnp.int32, (c, c), 0)
        q_i = lax.broadcasted_iota(jnp.int32, (c, c), 1)
        causal = r >= q_i
        dpos = (r - q_i).astype(F32)
        tok = lax.broadcasted_iota(jnp.int32, (c, LANES), 0).astype(F32)
        for h in range(RET_HEADS):
            lg = math.log(1.0 - 2.0 ** (-5.0 - h))
            dmat = jnp.where(causal, jnp.exp(jnp.where(causal, dpos * lg, 0.0)), 0.0)
            dmat_ref[h] = dmat * (RET_DK ** -0.5)
            scale_ref[0, h] = jnp.exp((tok + 1.0) * lg)
            scale_ref[1, h] = jnp.exp((c - 1.0 - tok) * lg) * (RET_DK ** -0.5)

    cos = cos_ref[...]
    sin = sin_ref[...]
    half = RET_DK // 2

    def rotate(x):
        xe, xo = x[:, 0:half], x[:, half:RET_DK]
        return jnp.concatenate([xe * cos - xo * sin, xo * cos + xe * sin], axis=1)

    both = lambda t: jnp.concatenate([t, t], axis=1)
    lanes = lambda h: slice(h * RET_DK, (h + 1) * RET_DK)
    qr = [rotate(q_ref[i, :, lanes(h)].astype(F32)) for i, h in probs]
    kr = [rotate(k_ref[i, :, lanes(h)].astype(F32)) for i, h in probs]
    attn = [_bdot_nt(qr[j], kr[j]) * dmat_ref[h] for j, (i, h) in enumerate(probs)]
    o = [_bdot(qr[j] * both(scale_ref[0, h]), s_ref[i, h]) + _bdot(attn[j], v_ref[i, :, lanes(h)])
         for j, (i, h) in enumerate(probs)]
    ds = [_bdot_tn(kr[j] * both(scale_ref[1, h]), v_ref[i, :, lanes(h)]) for j, (i, h) in enumerate(probs)]
    for j, (i, h) in enumerate(probs):
        s_ref[i, h] = s_ref[i, h] * math.exp(c * math.log(1.0 - 2.0 ** (-5.0 - h))) + ds[j]
        mu = jnp.mean(o[j], axis=-1, keepdims=True)
        dev = o[j] - mu
        var = jnp.mean(jnp.square(dev), axis=-1, keepdims=True)
        out = dev * lax.rsqrt(var + GN_EPS) * gnw_ref[:, lanes(h)]
        out = out * _silu(g_ref[i, :, lanes(h)].astype(F32))
        o_ref[i, :, lanes(h)] = out.astype(BF16)

    @pl.when(pl.program_id(1) == n_blocks - 1)
    def _():
        for i, h in probs:
            s_ref[i, h] = _select_rows_exact(split_ref[...], s_ref[i, h])


def _ret_mixer(p3, cos, sin, gnw_row, s0):
    batch, seq, _ = p3.shape
    c = min(RET_CHUNK, seq)
    bb = min(RET_SEQS_PER_STEP, batch)
    split = _pair_split_matrix()
    col = lambda j: pl.BlockSpec((bb, c, RET_QK), lambda b, s: (b, s, j))
    state_spec = pl.BlockSpec((bb, RET_HEADS, RET_DK, RET_DV), lambda b, s: (b, 0, 0, 0))
    square = pl.BlockSpec((RET_DK, RET_DK), lambda b, s: (0, 0))
    return pl.pallas_call(
        functools.partial(_ret_kernel, c=c, bb=bb, n_blocks=seq // c),
        grid=(batch // bb, seq // c),
        in_specs=[
            col(4), col(5), col(6), col(7),
            pl.BlockSpec((c, LANES), lambda b, s: (s, 0)),
            pl.BlockSpec((c, LANES), lambda b, s: (s, 0)),
            pl.BlockSpec((1, RET_V), lambda b, s: (0, 0)),
            state_spec,
            square,
            square,
        ],
        out_specs=[
            pl.BlockSpec((bb, c, RET_V), lambda b, s: (b, s, 0)),
            state_spec,
        ],
        out_shape=[
            jax.ShapeDtypeStruct((batch, seq, RET_V), BF16),
            jax.ShapeDtypeStruct((batch, RET_HEADS, RET_DK, RET_DV), F32),
        ],
        scratch_shapes=[pltpu.VMEM((RET_HEADS, c, c), F32), pltpu.VMEM((2, RET_HEADS, c, LANES), F32)],
        compiler_params=_params(2),
        name="ret_mixer",
    )(p3, p3, p3, p3, cos, sin, gnw_row, s0, jnp.asarray(split, dtype=BF16), jnp.asarray(split.T, dtype=BF16))


def _out_proj_kernel(x_ref, ma_ref, mr_ref, woa_ref, wor_ref, nw_ref, x1_ref, h2_ref):
    x1 = x_ref[...] + jnp.dot(ma_ref[...], woa_ref[...], preferred_element_type=F32)
    x1 = x1 + jnp.dot(mr_ref[...], wor_ref[...], preferred_element_type=F32)
    x1_ref[...] = x1
    h2_ref[...] = (x1 * _rms_scale(x1) * nw_ref[...]).astype(BF16)


def _out_proj(x2d, mix_a, mix_r, w_out_a, w_out_r, norm_w):
    t, d = x2d.shape
    tm = min(OUT_PROJ_TM, t)
    return pl.pallas_call(
        _out_proj_kernel,
        grid=(t // tm,),
        in_specs=[
            pl.BlockSpec((tm, d), lambda i: (i, 0)),
            pl.BlockSpec((tm, GDN_V), lambda i: (i, 0)),
            pl.BlockSpec((tm, RET_V), lambda i: (i, 0)),
            pl.BlockSpec((GDN_V, d), lambda i: (0, 0)),
            pl.BlockSpec((RET_V, d), lambda i: (0, 0)),
            pl.BlockSpec((1, d), lambda i: (0, 0)),
        ],
        out_specs=[
            pl.BlockSpec((tm, d), lambda i: (i, 0)),
            pl.BlockSpec((tm, d), lambda i: (i, 0)),
        ],
        out_shape=[
            jax.ShapeDtypeStruct((t, d), F32),
            jax.ShapeDtypeStruct((t, d), BF16),
        ],
        compiler_params=_params(1),
        name="out_proj",
    )(x2d, mix_a, mix_r, w_out_a, w_out_r, norm_w)


def _ffn_kernel(h2_ref, x1_ref, wg_ref, wu_ref, wd_ref, fw_ref, y_ref, *, n_ff, final_norm):
    f = pl.program_id(1)

    @pl.when(f == 0)
    def _():
        y_ref[...] = x1_ref[...]

    h = h2_ref[...]
    gate = jnp.dot(h, wg_ref[...], preferred_element_type=F32)
    up = jnp.dot(h, wu_ref[...], preferred_element_type=F32)
    act = (_silu(gate) * up).astype(BF16)
    y_ref[...] += jnp.dot(act, wd_ref[...], preferred_element_type=F32)

    if final_norm:
        @pl.when(f == n_ff - 1)
        def _():
            x2 = y_ref[...]
            y_ref[...] = x2 * _rms_scale(x2) * fw_ref[...]


def _ffn(h2, x1, w_gate_up, w_down, final_w, final_norm):
    t, d = x1.shape
    d_ff = w_down.shape[0]
    tm = min(FFN_TM, t)
    tf = FFN_TF
    n_ff = d_ff // tf
    return pl.pallas_call(
        functools.partial(_ffn_kernel, n_ff=n_ff, final_norm=final_norm),
        grid=(t // tm, n_ff),
        in_specs=[
            pl.BlockSpec((tm, d), lambda i, f: (i, 0)),
            pl.BlockSpec((tm, d), lambda i, f: (i, 0)),
            pl.BlockSpec((d, tf), lambda i, f: (0, f)),
            pl.BlockSpec((d, tf), lambda i, f: (0, n_ff + f)),
            pl.BlockSpec((tf, d), lambda i, f: (f, 0)),
            pl.BlockSpec((1, d), lambda i, f: (0, 0)),
        ],
        out_specs=pl.BlockSpec((tm, d), lambda i, f: (i, 0)),
        out_shape=jax.ShapeDtypeStruct((t, d), F32),
        compiler_params=_params(2, VMEM_LIMIT_BIG_TILE_BYTES),
        name="ffn",
    )(h2, x1, w_gate_up, w_gate_up, w_down, final_w)


W_IN_PREP_ROWS = 256


def _w_in_layout_kernel(w_ref, split_ref, main_ref, ba_ref):
    o_b = GDN_CONV_DIM + GDN_V
    o_qr = o_b + 2 * GDN_HEADS
    main_ref[:, 0:o_b] = w_ref[:, 0:o_b].astype(BF16)
    for head in range(2 * RET_HEADS):
        src = w_ref[:, o_qr + head * RET_DK:o_qr + (head + 1) * RET_DK].astype(BF16)
        main_ref[:, o_b + head * RET_DK:o_b + (head + 1) * RET_DK] = jnp.dot(
            src, split_ref[...], preferred_element_type=F32).astype(BF16)
    rest = o_b + 2 * RET_QK
    main_ref[:, rest:P_WIDTH] = w_ref[:, o_qr + 2 * RET_QK:o_qr + 2 * RET_QK + P_WIDTH - rest].astype(BF16)
    gates = w_ref[:, o_b:o_qr]
    pad = jnp.zeros((gates.shape[0], LANES - GDN_HEADS), F32)
    ba_ref[...] = jnp.concatenate([gates[:, 0:GDN_HEADS], pad, gates[:, GDN_HEADS:2 * GDN_HEADS], pad],
                                  axis=1).astype(BF16)


def _w_in_layout(w_in):
    d, width = w_in.shape
    rows = min(W_IN_PREP_ROWS, d)
    return pl.pallas_call(
        _w_in_layout_kernel,
        grid=(d // rows,),
        in_specs=[pl.BlockSpec((rows, width), lambda i: (i, 0)),
                  pl.BlockSpec((RET_DK, RET_DK), lambda i: (0, 0))],
        out_specs=[pl.BlockSpec((rows, P_WIDTH), lambda i: (i, 0)),
                   pl.BlockSpec((rows, BA_WIDTH), lambda i: (i, 0))],
        out_shape=[jax.ShapeDtypeStruct((d, P_WIDTH), BF16), jax.ShapeDtypeStruct((d, BA_WIDTH), BF16)],
        compiler_params=_params(1),
        name="w_in_layout",
    )(w_in, jnp.asarray(_pair_split_matrix(), dtype=BF16))


def _prep_layer_weights(w_in, w_out, w_gate_up, w_down, a_log, dt_bias, gdn_norm_w, ret_gn_w,
                        attn_norm_w, ffn_norm_w, conv_w):
    d = w_in.shape[0]
    w_main, w_ba = _w_in_layout(w_in)
    row_pad = lambda v: jnp.pad(v.astype(F32), (0, LANES - v.shape[0])).reshape(1, LANES)
    return dict(
        w_main=w_main, w_ba=w_ba,
        w_out_a=w_out[0:GDN_V].astype(BF16), w_out_r=w_out[GDN_V:].astype(BF16),
        w_gate_up=w_gate_up.astype(BF16), w_down=w_down.astype(BF16),
        alog_row=row_pad(a_log), dtb_row=row_pad(dt_bias),
        gnw_row=gdn_norm_w.reshape(1, GDN_DV).astype(F32),
        ret_gnw_row=ret_gn_w.reshape(1, RET_V).astype(F32),
        attn_norm_w=attn_norm_w.reshape(1, d), ffn_norm_w=ffn_norm_w.reshape(1, d),
        conv_w=conv_w,
    )


def _layer(x2d, batch, seq, cos, sin, conv_state, s_gdn, s_ret, lw, final_w, final_norm):
    p, ba = _in_proj(x2d, lw["attn_norm_w"], lw["w_main"], lw["w_ba"])
    p3 = p.reshape(batch, seq, P_WIDTH)
    conv0 = jnp.pad(conv_state.astype(BF16), ((0, 0), (CARRY_ROWS - (CONV_W - 1), 0), (0, 0)))
    mix_a, s_gdn_new = _gdn_mixer(p3, ba.reshape(batch, seq, BA_WIDTH), lw["conv_w"], lw["alog_row"],
                                  lw["dtb_row"], lw["gnw_row"], conv0, s_gdn.astype(F32))
    mix_r, s_ret_new = _ret_mixer(p3, cos, sin, lw["ret_gnw_row"], s_ret.astype(F32))
    mix_a = mix_a.reshape(batch * seq, GDN_V)
    mix_r = mix_r.reshape(batch * seq, RET_V)
    x1, h2 = _out_proj(x2d, mix_a, mix_r, lw["w_out_a"], lw["w_out_r"], lw["ffn_norm_w"])
    y = _ffn(h2, x1, lw["w_gate_up"], lw["w_down"], final_w, final_norm)
    rows = p3[:, seq - (CONV_W - 1):, 0:GDN_CONV_DIM].astype(conv_state.dtype)
    return y, rows, s_gdn_new.astype(s_gdn.dtype), s_ret_new.astype(s_ret.dtype)


def kernel(x_prompt, x_sample, state_gdn_conv, state_gdn, state_ret, attn_norm_w, w_in, conv_w, a_log, dt_bias, gdn_norm_w, ret_gn_w, w_out, ffn_norm_w, w_gate_up, w_down, final_norm_w):
    depth = w_in.shape[0]
    bp, lp, d = x_prompt.shape
    bs, ls, _ = x_sample.shape
    assert lp >= CONV_W - 1 and ls >= CONV_W - 1
    inv = 1.0 / (ROPE_BASE ** jnp.linspace(0.0, 1.0, RET_DK // 2, dtype=F32))
    inv_row = inv.reshape(1, LANES)
    cos_p, sin_p = _rope_tables(inv_row, lp, 0)
    cos_s, sin_s = _rope_tables(inv_row, ls, PAST_LEN)
    final_w = final_norm_w.reshape(1, d)
    hp = x_prompt.reshape(bp * lp, d)
    hs = x_sample.reshape(bs * ls, d)
    outs = [[] for _ in range(6)]
    for l in range(depth):
        lw = _prep_layer_weights(w_in[l], w_out[l], w_gate_up[l], w_down[l], a_log[l], dt_bias[l],
                                 gdn_norm_w[l], ret_gn_w[l], attn_norm_w[l], ffn_norm_w[l], conv_w[l])
        last = l == depth - 1
        conv0 = jnp.zeros((bp, CONV_W - 1, GDN_CONV_DIM), x_prompt.dtype)
        sg0 = jnp.zeros((bp, GDN_HEADS, GDN_DK, GDN_DV), state_gdn.dtype)
        sr0 = jnp.zeros((bp, RET_HEADS, RET_DK, RET_DV), state_ret.dtype)
        hp, c_p, g_p, r_p = _layer(hp, bp, lp, cos_p, sin_p, conv0, sg0, sr0, lw, final_w, last)
        hs, c_s, g_s, r_s = _layer(hs, bs, ls, cos_s, sin_s, state_gdn_conv[l], state_gdn[l],
                                   state_ret[l], lw, final_w, last)
        for acc, val in zip(outs, (c_p, g_p, r_p, c_s, g_s, r_s)):
            acc.append(val)
    return (hp.reshape(bp, lp, d), hs.reshape(bs, ls, d)) + tuple(jnp.stack(o) for o in outs)
```

```python
import functools
import math

import jax
import jax.numpy as jnp
import numpy as np
from jax import lax
from jax.experimental import pallas as pl
from jax.experimental.pallas import tpu as pltpu

F32 = jnp.float32
BF16 = jnp.bfloat16

GDN_HEADS = 8
GDN_DK = 128
GDN_DV = 128
CONV_W = 4
RET_HEADS = 4
RET_DK = 256
RET_DV = 256
GDN_QK = GDN_HEADS * GDN_DK
GDN_V = GDN_HEADS * GDN_DV
GDN_CONV_DIM = 2 * GDN_QK + GDN_V
RET_QK = RET_HEADS * RET_DK
RET_V = RET_HEADS * RET_DV
CHUNK = 64
PAST_LEN = 4096
RMS_EPS = 1e-6
GN_EPS = 1e-5
L2_EPS = 1e-6
ROPE_BASE = 10000.0

LANES = 128
SUBLANES = 8
V7X_VMEM_BYTES = 64 * 1024 * 1024
VMEM_LIMIT_BYTES = 52 * 1024 * 1024
VMEM_LIMIT_BIG_TILE_BYTES = 60 * 1024 * 1024
TRI_BLOCK = 16

P_WIDTH = 8 * 1024
BA_WIDTH = 2 * LANES

IN_PROJ_TM, IN_PROJ_TN = 1024, 2048
OUT_PROJ_TM = 512
FFN_TM, FFN_TF = 1024, 512
FFN_TF_FEW_ROWS = 1408


def _bdot(a, b):
    return jnp.dot(a.astype(BF16), b.astype(BF16), preferred_element_type=F32)


def _bdot_nt(a, b):
    return lax.dot_general(a.astype(BF16), b.astype(BF16), (((1,), (1,)), ((), ())),
                           preferred_element_type=F32)


def _bdot_tn(a, b):
    return lax.dot_general(a.astype(BF16), b.astype(BF16), (((0,), (0,)), ((), ())),
                           preferred_element_type=F32)


def _select_rows_exact(sel, x):
    hi = x.astype(BF16)
    rest = x - hi.astype(F32)
    mid = rest.astype(BF16)
    lo = (rest - mid.astype(F32)).astype(BF16)
    n = x.shape[1]
    parts = jnp.dot(sel, jnp.concatenate([hi, mid, lo], axis=1), preferred_element_type=F32)
    return parts[:, 0:n] + parts[:, n:2 * n] + parts[:, 2 * n:3 * n]


def _sigmoid(x):
    return 1.0 / (1.0 + jnp.exp(-x))


def _silu(x):
    return x * _sigmoid(x)


def _rms_scale(x):
    return lax.rsqrt(jnp.mean(x * x, axis=-1, keepdims=True) + RMS_EPS)


def _params(n_grid_dims, vmem_limit_bytes=VMEM_LIMIT_BYTES):
    return pltpu.CompilerParams(dimension_semantics=("arbitrary",) * n_grid_dims,
                                vmem_limit_bytes=vmem_limit_bytes)


def _in_proj_kernel(x_ref, nw_ref, w_ref, wba_ref, p_ref, ba_ref, h_ref):
    @pl.when(pl.program_id(1) == 0)
    def _():
        x = x_ref[...]
        h = (x * _rms_scale(x) * nw_ref[...]).astype(BF16)
        h_ref[...] = h
        ba_ref[...] = jnp.dot(h, wba_ref[...], preferred_element_type=F32)

    p_ref[...] = jnp.dot(h_ref[...], w_ref[...], preferred_element_type=F32).astype(BF16)


def _in_proj(x2d, norm_w, w_main, w_ba):
    t, d = x2d.shape
    tm = min(IN_PROJ_TM, t)
    tn = IN_PROJ_TN
    return pl.pallas_call(
        _in_proj_kernel,
        grid=(t // tm, P_WIDTH // tn),
        in_specs=[
            pl.BlockSpec((tm, d), lambda i, j: (i, 0)),
            pl.BlockSpec((1, d), lambda i, j: (0, 0)),
            pl.BlockSpec((d, tn), lambda i, j: (0, j)),
            pl.BlockSpec((d, BA_WIDTH), lambda i, j: (0, 0)),
        ],
        out_specs=[
            pl.BlockSpec((tm, tn), lambda i, j: (i, j)),
            pl.BlockSpec((tm, BA_WIDTH), lambda i, j: (i, 0)),
        ],
        out_shape=[
            jax.ShapeDtypeStruct((t, P_WIDTH), BF16),
            jax.ShapeDtypeStruct((t, BA_WIDTH), F32),
        ],
        scratch_shapes=[pltpu.VMEM((tm, d), BF16)],
        compiler_params=_params(2, VMEM_LIMIT_BIG_TILE_BYTES),
        name="in_proj",
    )(x2d, norm_w, w_main, w_ba)


M_CAUSAL, M_STRICT, M_EYE, M_DIAG, M_MERGE0 = 0, 1, 2, 3, 4
CARRY_ROWS = 16
GDN_SEQS_PER_STEP = 4


def _gdn_constants(c):
    r = np.arange(c)[:, None]
    q = np.arange(c)[None, :]
    masks = [r >= q, r > q, r == q, (r // TRI_BLOCK) == (q // TRI_BLOCK)]
    blk = TRI_BLOCK
    while blk < c:
        masks.append(((r // (2 * blk)) == (q // (2 * blk))) & ((r // blk) != (q // blk)))
        blk *= 2
    masks = np.stack(masks).astype(np.float32)
    shift = np.zeros(((CONV_W - 1) * c, CARRY_ROWS + c), np.float32)
    for k in range(CONV_W - 1):
        shift[k * c + np.arange(c), CARRY_ROWS + np.arange(c) - (CONV_W - 1) + k] = 1.0
    return jnp.asarray(masks), jnp.asarray(shift, dtype=BF16)


def _tri_inv(a_list, masks_ref, c):
    pw = [a * masks_ref[M_DIAG] for a in a_list]
    p = [masks_ref[M_EYE] - d for d in pw]
    for _ in range(int(math.log2(TRI_BLOCK)) - 1):
        pw = [_bdot(x, x) for x in pw]
        p = [x + _bdot(x, y) for x, y in zip(p, pw)]
    blk, level = TRI_BLOCK, M_MERGE0
    while blk < c:
        t = [_bdot(x, a * masks_ref[level]) for x, a in zip(p, a_list)]
        p = [x - _bdot(y, x) for x, y in zip(p, t)]
        blk, level = 2 * blk, level + 1
    return p


def _gdn_kernel(qkv_ref, z_ref, ba_ref, convw_ref, alog_ref, dtb_ref, gnw_ref, conv0_ref, s0_ref,
                masks_ref, shift_ref, o_ref, s_ref, xbuf_ref, *, c, bb):
    @pl.when(pl.program_id(1) == 0)
    def _():
        xbuf_ref[:, 0:CARRY_ROWS, :] = conv0_ref[...]
        s_ref[...] = s0_ref[...]

    heads = range(GDN_HEADS)
    seqs = range(bb)
    col = lambda arr, h: arr[:, h:h + 1]
    lanes = lambda base, h: slice(base + h * GDN_DK, base + (h + 1) * GDN_DK)
    rows = lambda arr, h: arr[h * c:(h + 1) * c, :]

    def shifted_rows(i):
        xbuf_ref[i, CARRY_ROWS:CARRY_ROWS + c, :] = qkv_ref[i]
        shifted = jnp.dot(shift_ref[...], xbuf_ref[i], preferred_element_type=F32)
        xbuf_ref[i, 0:CARRY_ROWS, :] = xbuf_ref[i, c:c + CARRY_ROWS, :]
        return shifted

    def gates(i):
        beta = _sigmoid(ba_ref[i, :, 0:LANES])
        sp_in = ba_ref[i, :, LANES:2 * LANES] + dtb_ref[...]
        softplus = jnp.maximum(sp_in, 0.0) + jnp.log(1.0 + jnp.exp(-jnp.abs(sp_in)))
        gc = _select_rows_exact(masks_ref[M_CAUSAL].astype(BF16), -jnp.exp(alog_ref[...]) * softplus)
        gc_t = jnp.concatenate([gc, jnp.zeros((LANES - c, LANES), F32)], axis=0).T
        g_last = gc[c - 1:c, :]
        e_gc = jnp.exp(gc)
        return dict(beta=beta, gc=gc, gc_t=gc_t, e_gc=e_gc, e_kd=jnp.exp(g_last - gc),
                    e_last=jnp.exp(g_last), beta_e=beta * e_gc)

    def conv_taps(i, shifted):
        w = convw_ref[...]
        y = shifted[0:c, :] * w[0:1, :]
        y = y + shifted[c:2 * c, :] * w[1:2, :]
        y = y + shifted[2 * c:3 * c, :] * w[2:3, :]
        y = y + qkv_ref[i].astype(F32) * w[3:4, :]
        return _silu(y)

    def inv_norm(y, base, fill):
        sq = jnp.concatenate([jnp.square(y[:, lanes(base, h)]) for h in heads], axis=0)
        tot = _bdot(sq, jnp.full((GDN_DK, LANES), fill, F32))
        return lax.rsqrt(tot + fill * L2_EPS)

    shifted = [shifted_rows(i) for i in seqs]
    pre = [gates(i) for i in seqs]
    y = [conv_taps(i, shifted[i]) for i in seqs]
    inv_q = [inv_norm(y[i], 0, float(GDN_DK)) for i in seqs]
    inv_k = [inv_norm(y[i], GDN_QK, 1.0) for i in seqs]
    probs = [(i, h) for i in seqs for h in heads]
    qn = [y[i][:, lanes(0, h)] * rows(inv_q[i], h) for i, h in probs]
    kn = [y[i][:, lanes(GDN_QK, h)] * rows(inv_k[i], h) for i, h in probs]
    kb = [kn[j] * col(pre[i]["beta"], h) for j, (i, h) in enumerate(probs)]
    qk = [_bdot_nt(jnp.concatenate([qn[j], kb[j]], axis=0), kn[j]) for j in range(len(probs))]
    attn, a_mat = [], []
    for j, (i, h) in enumerate(probs):
        diff = col(pre[i]["gc"], h) - pre[i]["gc_t"][h:h + 1, 0:c]
        decay = jnp.exp(jnp.minimum(diff, 0.0))
        attn.append(qk[j][0:c, :] * (decay * masks_ref[M_CAUSAL]))
        a_mat.append(qk[j][c:2 * c, :] * (decay * masks_ref[M_STRICT]))
    t_inv = _tri_inv(a_mat, masks_ref, c)
    w_u = [_bdot(t_inv[j], jnp.concatenate([y[i][:, lanes(2 * GDN_QK, h)] * col(pre[i]["beta"], h),
                                            kn[j] * col(pre[i]["beta_e"], h)], axis=1))
           for j, (i, h) in enumerate(probs)]
    x_s = [_bdot(jnp.concatenate([w_u[j][:, GDN_DV:2 * GDN_DV], qn[j] * col(pre[i]["e_gc"], h)], axis=0),
                 s_ref[i, h]) for j, (i, h) in enumerate(probs)]
    u = [w_u[j][:, 0:GDN_DV] - x_s[j][0:c, :] for j in range(len(probs))]
    o = [x_s[j][c:2 * c, :] + _bdot(attn[j], u[j]) for j in range(len(probs))]
    ds = [_bdot_tn(kn[j] * col(pre[i]["e_kd"], h), u[j]) for j, (i, h) in enumerate(probs)]
    gnw = gnw_ref[...]
    for j, (i, h) in enumerate(probs):
        s_ref[i, h] = s_ref[i, h] * col(pre[i]["e_last"], h) + ds[j]
        out = o[j] * _rms_scale(o[j]) * gnw
        out = out * _silu(z_ref[i, :, lanes(0, h)].astype(F32))
        o_ref[i, :, lanes(0, h)] = out.astype(BF16)


def _gdn_mixer(p3, ba3, conv_w, alog_row, dtb_row, gnw_row, conv0, s0):
    batch, seq, _ = p3.shape
    c = min(CHUNK, seq)
    bb = min(GDN_SEQS_PER_STEP, batch)
    masks, shift = _gdn_constants(c)
    const = lambda shape: pl.BlockSpec(shape, lambda b, s: (0,) * len(shape))
    state_spec = pl.BlockSpec((bb, GDN_HEADS, GDN_DK, GDN_DV), lambda b, s: (b, 0, 0, 0))
    return pl.pallas_call(
        functools.partial(_gdn_kernel, c=c, bb=bb),
        grid=(batch // bb, seq // c),
        in_specs=[
            pl.BlockSpec((bb, c, GDN_CONV_DIM), lambda b, s: (b, s, 0)),
            pl.BlockSpec((bb, c, GDN_V), lambda b, s: (b, s, GDN_CONV_DIM // GDN_V)),
            pl.BlockSpec((bb, c, BA_WIDTH), lambda b, s: (b, s, 0)),
            const((CONV_W, GDN_CONV_DIM)),
            const((1, LANES)),
            const((1, LANES)),
            const((1, GDN_DV)),
            pl.BlockSpec((bb, CARRY_ROWS, GDN_CONV_DIM), lambda b, s: (b, 0, 0)),
            state_spec,
            const(masks.shape),
            const(shift.shape),
        ],
        out_specs=[
            pl.BlockSpec((bb, c, GDN_V), lambda b, s: (b, s, 0)),
            state_spec,
        ],
        out_shape=[
            jax.ShapeDtypeStruct((batch, seq, GDN_V), BF16),
            jax.ShapeDtypeStruct((batch, GDN_HEADS, GDN_DK, GDN_DV), F32),
        ],
        scratch_shapes=[pltpu.VMEM((bb, CARRY_ROWS + c, GDN_CONV_DIM), BF16)],
        compiler_params=_params(2),
        name="gdn_mixer",
    )(p3, p3, ba3, conv_w, alog_row, dtb_row, gnw_row, conv0, s0, masks, shift)


def _rope_kernel(inv_ref, cos_ref, sin_ref, *, rows, offset):
    pos = lax.broadcasted_iota(jnp.int32, (rows, LANES), 0) + (pl.program_id(0) * rows + offset)
    ang = pos.astype(F32) * inv_ref[...]
    cos_ref[...] = jnp.cos(ang)
    sin_ref[...] = jnp.sin(ang)


def _rope_tables(inv_row, seq, offset):
    rows = min(512, seq)
    return pl.pallas_call(
        functools.partial(_rope_kernel, rows=rows, offset=offset),
        grid=(seq // rows,),
        in_specs=[pl.BlockSpec((1, LANES), lambda i: (0, 0))],
        out_specs=[pl.BlockSpec((rows, LANES), lambda i: (i, 0))] * 2,
        out_shape=[jax.ShapeDtypeStruct((seq, LANES), F32)] * 2,
        compiler_params=_params(1),
        name="rope_tables",
    )(inv_row)


RET_CHUNK = 256
RET_SEQS_PER_STEP = 2


def _pair_split_matrix():
    perm = np.concatenate([np.arange(0, RET_DK, 2), np.arange(1, RET_DK, 2)])
    return np.eye(RET_DK, dtype=np.float32)[:, perm]


def _ret_kernel(q_ref, k_ref, v_ref, g_ref, cos_ref, sin_ref, gnw_ref, s0_ref, split_ref, split_t_ref,
                o_ref, s_ref, dmat_ref, scale_ref, *, c, bb, n_blocks):
    first_block = pl.program_id(1) == 0
    probs = [(i, h) for i in range(bb) for h in range(RET_HEADS)]

    @pl.when(first_block)
    def _():
        for i, h in probs:
            s_ref[i, h] = _select_rows_exact(split_t_ref[...], s0_ref[i, h])

    @pl.when(jnp.logical_and(pl.program_id(0) == 0, first_block))
    def _():
        r = lax.broadcasted_iota(jnp.int32, (c, c), 0)
        q_i = lax.broadcasted_iota(jnp.int32, (c, c), 1)
        causal = r >= q_i
        dpos = (r - q_i).astype(F32)
        tok = lax.broadcasted_iota(jnp.int32, (c, LANES), 0).astype(F32)
        for h in range(RET_HEADS):
            lg = math.log(1.0 - 2.0 ** (-5.0 - h))
            dmat = jnp.where(causal, jnp.exp(jnp.where(causal, dpos * lg, 0.0)), 0.0)
            dmat_ref[h] = dmat * (RET_DK ** -0.5)
            scale_ref[0, h] = jnp.exp((tok + 1.0) * lg)
            scale_ref[1, h] = jnp.exp((c - 1.0 - tok) * lg) * (RET_DK ** -0.5)

    cos = cos_ref[...]
    sin = sin_ref[...]
    half = RET_DK // 2

    def rotate(x):
        xe, xo = x[:, 0:half], x[:, half:RET_DK]
        return jnp.concatenate([xe * cos - xo * sin, xo * cos + xe * sin], axis=1)

    both = lambda t: jnp.concatenate([t, t], axis=1)
    lanes = lambda h: slice(h * RET_DK, (h + 1) * RET_DK)
    qr = [rotate(q_ref[i, :, lanes(h)].astype(F32)) for i, h in probs]
    kr = [rotate(k_ref[i, :, lanes(h)].astype(F32)) for i, h in probs]
    attn = [_bdot_nt(qr[j], kr[j]) * dmat_ref[h] for j, (i, h) in enumerate(probs)]
    o = [_bdot(qr[j] * both(scale_ref[0, h]), s_ref[i, h]) + _bdot(attn[j], v_ref[i, :, lanes(h)])
         for j, (i, h) in enumerate(probs)]
    ds = [_bdot_tn(kr[j] * both(scale_ref[1, h]), v_ref[i, :, lanes(h)]) for j, (i, h) in enumerate(probs)]
    for j, (i, h) in enumerate(probs):
        s_ref[i, h] = s_ref[i, h] * math.exp(c * math.log(1.0 - 2.0 ** (-5.0 - h))) + ds[j]
        mu = jnp.mean(o[j], axis=-1, keepdims=True)
        dev = o[j] - mu
        var = jnp.mean(jnp.square(dev), axis=-1, keepdims=True)
        out = dev * lax.rsqrt(var + GN_EPS) * gnw_ref[:, lanes(h)]
        out = out * _silu(g_ref[i, :, lanes(h)].astype(F32))
        o_ref[i, :, lanes(h)] = out.astype(BF16)

    @pl.when(pl.program_id(1) == n_blocks - 1)
    def _():
        for i, h in probs:
            s_ref[i, h] = _select_rows_exact(split_ref[...], s_ref[i, h])


def _ret_mixer(p3, cos, sin, gnw_row, s0):
    batch, seq, _ = p3.shape
    c = min(RET_CHUNK, seq)
    bb = min(RET_SEQS_PER_STEP, batch)
    split = _pair_split_matrix()
    col = lambda j: pl.BlockSpec((bb, c, RET_QK), lambda b, s: (b, s, j))
    state_spec = pl.BlockSpec((bb, RET_HEADS, RET_DK, RET_DV), lambda b, s: (b, 0, 0, 0))
    square = pl.BlockSpec((RET_DK, RET_DK), lambda b, s: (0, 0))
    return pl.pallas_call(
        functools.partial(_ret_kernel, c=c, bb=bb, n_blocks=seq // c),
        grid=(batch // bb, seq // c),
        in_specs=[
            col(4), col(5), col(6), col(7),
            pl.BlockSpec((c, LANES), lambda b, s: (s, 0)),
            pl.BlockSpec((c, LANES), lambda b, s: (s, 0)),
            pl.BlockSpec((1, RET_V), lambda b, s: (0, 0)),
            state_spec,
            square,
            square,
        ],
        out_specs=[
            pl.BlockSpec((bb, c, RET_V), lambda b, s: (b, s, 0)),
            state_spec,
        ],
        out_shape=[
            jax.ShapeDtypeStruct((batch, seq, RET_V), BF16),
            jax.ShapeDtypeStruct((batch, RET_HEADS, RET_DK, RET_DV), F32),
        ],
        scratch_shapes=[pltpu.VMEM((RET_HEADS, c, c), F32), pltpu.VMEM((2, RET_HEADS, c, LANES), F32)],
        compiler_params=_params(2),
        name="ret_mixer",
    )(p3, p3, p3, p3, cos, sin, gnw_row, s0, jnp.asarray(split, dtype=BF16), jnp.asarray(split.T, dtype=BF16))


def _out_proj_kernel(x_ref, ma_ref, mr_ref, woa_ref, wor_ref, nw_ref, x1_ref, h2_ref):
    x1 = x_ref[...] + jnp.dot(ma_ref[...], woa_ref[...], preferred_element_type=F32)
    x1 = x1 + jnp.dot(mr_ref[...], wor_ref[...], preferred_element_type=F32)
    x1_ref[...] = x1
    h2_ref[...] = (x1 * _rms_scale(x1) * nw_ref[...]).astype(BF16)


def _out_proj(x2d, mix_a, mix_r, w_out_a, w_out_r, norm_w):
    t, d = x2d.shape
    tm = min(OUT_PROJ_TM, t)
    return pl.pallas_call(
        _out_proj_kernel,
        grid=(t // tm,),
        in_specs=[
            pl.BlockSpec((tm, d), lambda i: (i, 0)),
            pl.BlockSpec((tm, GDN_V), lambda i: (i, 0)),
            pl.BlockSpec((tm, RET_V), lambda i: (i, 0)),
            pl.BlockSpec((GDN_V, d), lambda i: (0, 0)),
            pl.BlockSpec((RET_V, d), lambda i: (0, 0)),
            pl.BlockSpec((1, d), lambda i: (0, 0)),
        ],
        out_specs=[
            pl.BlockSpec((tm, d), lambda i: (i, 0)),
            pl.BlockSpec((tm, d), lambda i: (i, 0)),
        ],
        out_shape=[
            jax.ShapeDtypeStruct((t, d), F32),
            jax.ShapeDtypeStruct((t, d), BF16),
        ],
        compiler_params=_params(1),
        name="out_proj",
    )(x2d, mix_a, mix_r, w_out_a, w_out_r, norm_w)


def _ffn_kernel(h2_ref, x1_ref, wg_ref, wu_ref, wd_ref, fw_ref, y_ref, *, n_ff, final_norm):
    f = pl.program_id(1)

    @pl.when(f == 0)
    def _():
        y_ref[...] = x1_ref[...]

    h = h2_ref[...]
    gate = jnp.dot(h, wg_ref[...], preferred_element_type=F32)
    up = jnp.dot(h, wu_ref[...], preferred_element_type=F32)
    act = (_silu(gate) * up).astype(BF16)
    y_ref[...] += jnp.dot(act, wd_ref[...], preferred_element_type=F32)

    if final_norm:
        @pl.when(f == n_ff - 1)
        def _():
            x2 = y_ref[...]
            y_ref[...] = x2 * _rms_scale(x2) * fw_ref[...]


def _ffn(h2, x1, w_gate_up, w_down, final_w, final_norm):
    t, d = x1.shape
    d_ff = w_down.shape[0]
    tm = min(FFN_TM, t)
    tf = FFN_TF if tm == FFN_TM else FFN_TF_FEW_ROWS
    n_ff = d_ff // tf
    return pl.pallas_call(
        functools.partial(_ffn_kernel, n_ff=n_ff, final_norm=final_norm),
        grid=(t // tm, n_ff),
        in_specs=[
            pl.BlockSpec((tm, d), lambda i, f: (i, 0)),
            pl.BlockSpec((tm, d), lambda i, f: (i, 0)),
            pl.BlockSpec((d, tf), lambda i, f: (0, f)),
            pl.BlockSpec((d, tf), lambda i, f: (0, n_ff + f)),
            pl.BlockSpec((tf, d), lambda i, f: (f, 0)),
            pl.BlockSpec((1, d), lambda i, f: (0, 0)),
        ],
        out_specs=pl.BlockSpec((tm, d), lambda i, f: (i, 0)),
        out_shape=jax.ShapeDtypeStruct((t, d), F32),
        compiler_params=_params(2, VMEM_LIMIT_BIG_TILE_BYTES),
        name="ffn",
    )(h2, x1, w_gate_up, w_gate_up, w_down, final_w)


W_IN_PREP_ROWS = 256


def _w_in_layout_kernel(w_ref, split_ref, main_ref, ba_ref):
    o_b = GDN_CONV_DIM + GDN_V
    o_qr = o_b + 2 * GDN_HEADS
    main_ref[:, 0:o_b] = w_ref[:, 0:o_b]
    for head in range(2 * RET_HEADS):
        src = w_ref[:, o_qr + head * RET_DK:o_qr + (head + 1) * RET_DK]
        main_ref[:, o_b + head * RET_DK:o_b + (head + 1) * RET_DK] = jnp.dot(
            src, split_ref[...], preferred_element_type=F32).astype(BF16)
    rest = o_b + 2 * RET_QK
    main_ref[:, rest:P_WIDTH] = w_ref[:, o_qr + 2 * RET_QK:o_qr + 2 * RET_QK + P_WIDTH - rest]
    gates = w_ref[:, o_b:o_qr]
    pad = jnp.zeros((gates.shape[0], LANES - GDN_HEADS), BF16)
    ba_ref[...] = jnp.concatenate([gates[:, 0:GDN_HEADS], pad, gates[:, GDN_HEADS:2 * GDN_HEADS], pad], axis=1)


def _w_in_layout(w_in_bf16):
    d, width = w_in_bf16.shape
    rows = min(W_IN_PREP_ROWS, d)
    return pl.pallas_call(
        _w_in_layout_kernel,
        grid=(d // rows,),
        in_specs=[pl.BlockSpec((rows, width), lambda i: (i, 0)),
                  pl.BlockSpec((RET_DK, RET_DK), lambda i: (0, 0))],
        out_specs=[pl.BlockSpec((rows, P_WIDTH), lambda i: (i, 0)),
                   pl.BlockSpec((rows, BA_WIDTH), lambda i: (i, 0))],
        out_shape=[jax.ShapeDtypeStruct((d, P_WIDTH), BF16), jax.ShapeDtypeStruct((d, BA_WIDTH), BF16)],
        compiler_params=_params(1),
        name="w_in_layout",
    )(w_in_bf16, jnp.asarray(_pair_split_matrix(), dtype=BF16))


def _prep_layer_weights(w_in, w_out, w_gate_up, w_down, a_log, dt_bias, gdn_norm_w, ret_gn_w,
                        attn_norm_w, ffn_norm_w, conv_w):
    d = w_in.shape[0]
    w_main, w_ba = _w_in_layout(w_in.astype(BF16))
    row_pad = lambda v: jnp.pad(v.astype(F32), (0, LANES - v.shape[0])).reshape(1, LANES)
    return dict(
        w_main=w_main, w_ba=w_ba,
        w_out_a=w_out[0:GDN_V].astype(BF16), w_out_r=w_out[GDN_V:].astype(BF16),
        w_gate_up=w_gate_up.astype(BF16), w_down=w_down.astype(BF16),
        alog_row=row_pad(a_log), dtb_row=row_pad(dt_bias),
        gnw_row=gdn_norm_w.reshape(1, GDN_DV).astype(F32),
        ret_gnw_row=ret_gn_w.reshape(1, RET_V).astype(F32),
        attn_norm_w=attn_norm_w.reshape(1, d), ffn_norm_w=ffn_norm_w.reshape(1, d),
        conv_w=conv_w,
    )


def _layer(x2d, batch, seq, cos, sin, conv_state, s_gdn, s_ret, lw, final_w, final_norm):
    p, ba = _in_proj(x2d, lw["attn_norm_w"], lw["w_main"], lw["w_ba"])
    p3 = p.reshape(batch, seq, P_WIDTH)
    conv0 = jnp.pad(conv_state.astype(BF16), ((0, 0), (CARRY_ROWS - (CONV_W - 1), 0), (0, 0)))
    mix_a, s_gdn_new = _gdn_mixer(p3, ba.reshape(batch, seq, BA_WIDTH), lw["conv_w"], lw["alog_row"],
                                  lw["dtb_row"], lw["gnw_row"], conv0, s_gdn.astype(F32))
    mix_r, s_ret_new = _ret_mixer(p3, cos, sin, lw["ret_gnw_row"], s_ret.astype(F32))
    mix_a = mix_a.reshape(batch * seq, GDN_V)
    mix_r = mix_r.reshape(batch * seq, RET_V)
    x1, h2 = _out_proj(x2d, mix_a, mix_r, lw["w_out_a"], lw["w_out_r"], lw["ffn_norm_w"])
    y = _ffn(h2, x1, lw["w_gate_up"], lw["w_down"], final_w, final_norm)
    rows = p3[:, seq - (CONV_W - 1):, 0:GDN_CONV_DIM].astype(conv_state.dtype)
    return y, rows, s_gdn_new.astype(s_gdn.dtype), s_ret_new.astype(s_ret.dtype)


def kernel(x_prompt, x_sample, state_gdn_conv, state_gdn, state_ret, attn_norm_w, w_in, conv_w, a_log, dt_bias, gdn_norm_w, ret_gn_w, w_out, ffn_norm_w, w_gate_up, w_down, final_norm_w):
    depth = w_in.shape[0]
    bp, lp, d = x_prompt.shape
    bs, ls, _ = x_sample.shape
    assert lp >= CONV_W - 1 and ls >= CONV_W - 1
    inv = 1.0 / (ROPE_BASE ** jnp.linspace(0.0, 1.0, RET_DK // 2, dtype=F32))
    inv_row = inv.reshape(1, LANES)
    cos_p, sin_p = _rope_tables(inv_row, lp, 0)
    cos_s, sin_s = _rope_tables(inv_row, ls, PAST_LEN)
    final_w = final_norm_w.reshape(1, d)
    hp = x_prompt.reshape(bp * lp, d)
    hs = x_sample.reshape(bs * ls, d)
    outs = [[] for _ in range(6)]
    for l in range(depth):
        lw = _prep_layer_weights(w_in[l], w_out[l], w_gate_up[l], w_down[l], a_log[l], dt_bias[l],
                                 gdn_norm_w[l], ret_gn_w[l], attn_norm_w[l], ffn_norm_w[l], conv_w[l])
        last = l == depth - 1
        conv0 = jnp.zeros((bp, CONV_W - 1, GDN_CONV_DIM), x_prompt.dtype)
        sg0 = jnp.zeros((bp, GDN_HEADS, GDN_DK, GDN_DV), state_gdn.dtype)
        sr0 = jnp.zeros((bp, RET_HEADS, RET_DK, RET_DV), state_ret.dtype)
        hp, c_p, g_p, r_p = _layer(hp, bp, lp, cos_p, sin_p, conv0, sg0, sr0, lw, final_w, last)
        hs, c_s, g_s, r_s = _layer(hs, bs, ls, cos_s, sin_s, state_gdn_conv[l], state_gdn[l],
                                   state_ret[l], lw, final_w, last)
        for acc, val in zip(outs, (c_p, g_p, r_p, c_s, g_s, r_s)):
            acc.append(val)
    return (hp.reshape(bp, lp, d), hs.reshape(bs, ls, d)) + tuple(jnp.stack(o) for o in outs)
```

```python
import functools
import math

import jax
import jax.numpy as jnp
import numpy as np
from jax import lax
from jax.experimental import pallas as pl
from jax.experimental.pallas import tpu as pltpu

F32 = jnp.float32
BF16 = jnp.bfloat16

GDN_HEADS = 8
GDN_DK = 128
GDN_DV = 128
CONV_W = 4
RET_HEADS = 4
RET_DK = 256
RET_DV = 256
GDN_QK = GDN_HEADS * GDN_DK
GDN_V = GDN_HEADS * GDN_DV
GDN_CONV_DIM = 2 * GDN_QK + GDN_V
RET_QK = RET_HEADS * RET_DK
RET_V = RET_HEADS * RET_DV
CHUNK = 64
PAST_LEN = 4096
RMS_EPS = 1e-6
GN_EPS = 1e-5
L2_EPS = 1e-6
ROPE_BASE = 10000.0

LANES = 128
SUBLANES = 8
V7X_VMEM_BYTES = 64 * 1024 * 1024
VMEM_LIMIT_BYTES = 52 * 1024 * 1024
VMEM_LIMIT_BIG_TILE_BYTES = 60 * 1024 * 1024
TRI_BLOCK = 16

P_WIDTH = 8 * 1024
BA_WIDTH = 2 * LANES

IN_PROJ_TM, IN_PROJ_TN = 1024, 2048
OUT_PROJ_TM = 512
FFN_TM, FFN_TF = 1024, 512
FFN_TF_FEW_ROWS = 1408


def _bdot(a, b):
    return jnp.dot(a.astype(BF16), b.astype(BF16), preferred_element_type=F32)


def _bdot_nt(a, b):
    return lax.dot_general(a.astype(BF16), b.astype(BF16), (((1,), (1,)), ((), ())),
                           preferred_element_type=F32)


def _bdot_tn(a, b):
    return lax.dot_general(a.astype(BF16), b.astype(BF16), (((0,), (0,)), ((), ())),
                           preferred_element_type=F32)


def _select_rows_exact(sel, x):
    hi = x.astype(BF16)
    rest = x - hi.astype(F32)
    mid = rest.astype(BF16)
    lo = (rest - mid.astype(F32)).astype(BF16)
    n = x.shape[1]
    parts = jnp.dot(sel, jnp.concatenate([hi, mid, lo], axis=1), preferred_element_type=F32)
    return parts[:, 0:n] + parts[:, n:2 * n] + parts[:, 2 * n:3 * n]


def _sigmoid(x):
    return 1.0 / (1.0 + jnp.exp(-x))


def _silu(x):
    return x * _sigmoid(x)


def _rms_scale(x):
    return lax.rsqrt(jnp.mean(x * x, axis=-1, keepdims=True) + RMS_EPS)


def _params(n_grid_dims, vmem_limit_bytes=VMEM_LIMIT_BYTES):
    return pltpu.CompilerParams(dimension_semantics=("arbitrary",) * n_grid_dims,
                                vmem_limit_bytes=vmem_limit_bytes)


def _in_proj_kernel(x_ref, nw_ref, w_ref, p_ref, h_ref):
    @pl.when(pl.program_id(1) == 0)
    def _():
        x = x_ref[...]
        h_ref[...] = (x * _rms_scale(x) * nw_ref[...]).astype(BF16)

    p_ref[...] = jnp.dot(h_ref[...], w_ref[...], preferred_element_type=F32).astype(BF16)


def _in_proj(x2d, norm_w, w_cols):
    t, d = x2d.shape
    width = w_cols.shape[1]
    tm = min(IN_PROJ_TM, t)
    tn = IN_PROJ_TN
    return pl.pallas_call(
        _in_proj_kernel,
        grid=(t // tm, width // tn),
        in_specs=[
            pl.BlockSpec((tm, d), lambda i, j: (i, 0)),
            pl.BlockSpec((1, d), lambda i, j: (0, 0)),
            pl.BlockSpec((d, tn), lambda i, j: (0, j)),
        ],
        out_specs=pl.BlockSpec((tm, tn), lambda i, j: (i, j)),
        out_shape=jax.ShapeDtypeStruct((t, width), BF16),
        scratch_shapes=[pltpu.VMEM((tm, d), BF16)],
        compiler_params=_params(2, VMEM_LIMIT_BIG_TILE_BYTES),
        name="in_proj",
    )(x2d, norm_w, w_cols)


M_CAUSAL, M_STRICT, M_EYE, M_DIAG, M_MERGE0 = 0, 1, 2, 3, 4
CARRY_ROWS = 16
GDN_SEQS_PER_STEP = 4


def _gdn_constants(c):
    r = np.arange(c)[:, None]
    q = np.arange(c)[None, :]
    masks = [r >= q, r > q, r == q, (r // TRI_BLOCK) == (q // TRI_BLOCK)]
    blk = TRI_BLOCK
    while blk < c:
        masks.append(((r // (2 * blk)) == (q // (2 * blk))) & ((r // blk) != (q // blk)))
        blk *= 2
    masks = np.stack(masks).astype(np.float32)
    shift = np.zeros(((CONV_W - 1) * c, CARRY_ROWS + c), np.float32)
    for k in range(CONV_W - 1):
        shift[k * c + np.arange(c), CARRY_ROWS + np.arange(c) - (CONV_W - 1) + k] = 1.0
    return jnp.asarray(masks), jnp.asarray(shift, dtype=BF16)


def _tri_inv(a_list, masks_ref, c):
    pw = [a * masks_ref[M_DIAG] for a in a_list]
    p = [masks_ref[M_EYE] - d for d in pw]
    for _ in range(int(math.log2(TRI_BLOCK)) - 1):
        pw = [_bdot(x, x) for x in pw]
        p = [x + _bdot(x, y) for x, y in zip(p, pw)]
    blk, level = TRI_BLOCK, M_MERGE0
    while blk < c:
        t = [_bdot(x, a * masks_ref[level]) for x, a in zip(p, a_list)]
        p = [x - _bdot(y, x) for x, y in zip(p, t)]
        blk, level = 2 * blk, level + 1
    return p


GDN_P_WIDTH = GDN_CONV_DIM + GDN_V
GDN_PROJ_TILE = 512


def _gdn_kernel(x_ref, nw_ref, w_ref, wba_ref, convw_ref, alog_ref, dtb_ref, gnw_ref, conv0_ref, s0_ref,
                masks_ref, shift_ref, o_ref, s_ref, tail_ref, xbuf_ref, p_buf, ba_buf, *, c, bb):
    t = pl.program_id(1)
    nslot = t % 2
    pslot = 1 - nslot

    @pl.when(t == 0)
    def _():
        p_buf[1] = jnp.zeros(p_buf.shape[1:], BF16)
        ba_buf[1] = jnp.zeros(ba_buf.shape[1:], F32)
        xbuf_ref[...] = jnp.zeros(xbuf_ref.shape, BF16)
        s_ref[...] = s0_ref[...]

    @pl.when(t == 1)
    def _():
        xbuf_ref[:, 0:CARRY_ROWS, :] = conv0_ref[...]
        s_ref[...] = s0_ref[...]

    heads = range(GDN_HEADS)
    seqs = range(bb)
    col = lambda arr, h: arr[:, h:h + 1]
    lanes = lambda base, h: slice(base + h * GDN_DK, base + (h + 1) * GDN_DK)
    rows = lambda arr, h: arr[h * c:(h + 1) * c, :]
    seq_rows = lambda i: slice(i * c, (i + 1) * c)

    x = x_ref[...].reshape(bb * c, x_ref.shape[-1])
    h_next = (x * _rms_scale(x) * nw_ref[...]).astype(BF16)
    proj_tiles = [slice(lo, lo + GDN_PROJ_TILE) for lo in range(0, GDN_P_WIDTH, GDN_PROJ_TILE)]

    def project(tiles):
        for cols in tiles:
            p_buf[nslot, :, cols] = jnp.dot(h_next, w_ref[:, cols], preferred_element_type=F32).astype(BF16)

    ba_buf[nslot] = jnp.dot(h_next, wba_ref[...], preferred_element_type=F32)

    qkv = lambda i: p_buf[pslot, seq_rows(i), 0:GDN_CONV_DIM]

    def shifted_rows(i):
        xbuf_ref[i, CARRY_ROWS:CARRY_ROWS + c, :] = qkv(i)
        shifted = jnp.dot(shift_ref[...], xbuf_ref[i], preferred_element_type=F32)
        xbuf_ref[i, 0:CARRY_ROWS, :] = xbuf_ref[i, c:c + CARRY_ROWS, :]
        return shifted

    def gates(i):
        ba = ba_buf[pslot, seq_rows(i), :]
        beta = _sigmoid(ba[:, 0:LANES])
        sp_in = ba[:, LANES:2 * LANES] + dtb_ref[...]
        softplus = jnp.maximum(sp_in, 0.0) + jnp.log(1.0 + jnp.exp(-jnp.abs(sp_in)))
        gc = _select_rows_exact(masks_ref[M_CAUSAL].astype(BF16), -jnp.exp(alog_ref[...]) * softplus)
        gc_t = jnp.concatenate([gc, jnp.zeros((LANES - c, LANES), F32)], axis=0).T
        g_last = gc[c - 1:c, :]
        e_gc = jnp.exp(gc)
        return dict(beta=beta, gc=gc, gc_t=gc_t, e_gc=e_gc, e_kd=jnp.exp(g_last - gc),
                    e_last=jnp.exp(g_last), beta_e=beta * e_gc)

    def conv_taps(i, shifted):
        w = convw_ref[...]
        y = shifted[0:c, :] * w[0:1, :]
        y = y + shifted[c:2 * c, :] * w[1:2, :]
        y = y + shifted[2 * c:3 * c, :] * w[2:3, :]
        y = y + qkv(i).astype(F32) * w[3:4, :]
        return _silu(y)

    def inv_norm(y, base, fill):
        sq = jnp.concatenate([jnp.square(y[:, lanes(base, h)]) for h in heads], axis=0)
        tot = _bdot(sq, jnp.full((GDN_DK, LANES), fill, F32))
        return lax.rsqrt(tot + fill * L2_EPS)

    n_tiles = len(proj_tiles)
    shifted = [shifted_rows(i) for i in seqs]
    project(proj_tiles[0:n_tiles // 2])
    pre = [gates(i) for i in seqs]
    y = [conv_taps(i, shifted[i]) for i in seqs]
    project(proj_tiles[n_tiles // 2:])
    inv_q = [inv_norm(y[i], 0, float(GDN_DK)) for i in seqs]
    inv_k = [inv_norm(y[i], GDN_QK, 1.0) for i in seqs]
    probs = [(i, h) for i in seqs for h in heads]
    qn = [y[i][:, lanes(0, h)] * rows(inv_q[i], h) for i, h in probs]
    kn = [y[i][:, lanes(GDN_QK, h)] * rows(inv_k[i], h) for i, h in probs]
    kb = [kn[j] * col(pre[i]["beta"], h) for j, (i, h) in enumerate(probs)]
    qk = [_bdot_nt(jnp.concatenate([qn[j], kb[j]], axis=0), kn[j]) for j in range(len(probs))]
    attn, a_mat = [], []
    for j, (i, h) in enumerate(probs):
        diff = col(pre[i]["gc"], h) - pre[i]["gc_t"][h:h + 1, 0:c]
        decay = jnp.exp(jnp.minimum(diff, 0.0))
        attn.append(qk[j][0:c, :] * (decay * masks_ref[M_CAUSAL]))
        a_mat.append(qk[j][c:2 * c, :] * (decay * masks_ref[M_STRICT]))
    t_inv = _tri_inv(a_mat, masks_ref, c)
    w_u = [_bdot(t_inv[j], jnp.concatenate([y[i][:, lanes(2 * GDN_QK, h)] * col(pre[i]["beta"], h),
                                            kn[j] * col(pre[i]["beta_e"], h)], axis=1))
           for j, (i, h) in enumerate(probs)]
    x_s = [_bdot(jnp.concatenate([w_u[j][:, GDN_DV:2 * GDN_DV], qn[j] * col(pre[i]["e_gc"], h)], axis=0),
                 s_ref[i, h]) for j, (i, h) in enumerate(probs)]
    u = [w_u[j][:, 0:GDN_DV] - x_s[j][0:c, :] for j in range(len(probs))]
    o = [x_s[j][c:2 * c, :] + _bdot(attn[j], u[j]) for j in range(len(probs))]
    ds = [_bdot_tn(kn[j] * col(pre[i]["e_kd"], h), u[j]) for j, (i, h) in enumerate(probs)]
    gnw = gnw_ref[...]
    for j, (i, h) in enumerate(probs):
        s_ref[i, h] = s_ref[i, h] * col(pre[i]["e_last"], h) + ds[j]
        out = o[j] * _rms_scale(o[j]) * gnw
        z = p_buf[pslot, seq_rows(i), GDN_CONV_DIM + h * GDN_DV:GDN_CONV_DIM + (h + 1) * GDN_DV]
        out = out * _silu(z.astype(F32))
        o_ref[i, :, lanes(0, h)] = out.astype(BF16)
    tail_ref[...] = xbuf_ref[:, 0:CARRY_ROWS, :]


def _gdn_mixer(x3, norm_w, w_gdn, w_ba, conv_w, alog_row, dtb_row, gnw_row, conv0, s0):
    batch, seq, d = x3.shape
    c = min(CHUNK, seq)
    bb = min(GDN_SEQS_PER_STEP, batch)
    n = seq // c
    masks, shift = _gdn_constants(c)
    const = lambda shape: pl.BlockSpec(shape, lambda b, t: (0,) * len(shape))
    state_spec = pl.BlockSpec((bb, GDN_HEADS, GDN_DK, GDN_DV), lambda b, t: (b, 0, 0, 0))
    carry_spec = pl.BlockSpec((bb, CARRY_ROWS, GDN_CONV_DIM), lambda b, t: (b, 0, 0))
    return pl.pallas_call(
        functools.partial(_gdn_kernel, c=c, bb=bb),
        grid=(batch // bb, n + 1),
        in_specs=[
            pl.BlockSpec((bb, c, d), lambda b, t: (b, jnp.minimum(t, n - 1), 0)),
            const((1, d)),
            const(w_gdn.shape),
            const(w_ba.shape),
            const((CONV_W, GDN_CONV_DIM)),
            const((1, LANES)),
            const((1, LANES)),
            const((1, GDN_DV)),
            carry_spec,
            state_spec,
            const(masks.shape),
            const(shift.shape),
        ],
        out_specs=[
            pl.BlockSpec((bb, c, GDN_V), lambda b, t: (b, jnp.maximum(t - 1, 0), 0)),
            state_spec,
            carry_spec,
        ],
        out_shape=[
            jax.ShapeDtypeStruct((batch, seq, GDN_V), BF16),
            jax.ShapeDtypeStruct((batch, GDN_HEADS, GDN_DK, GDN_DV), F32),
            jax.ShapeDtypeStruct((batch, CARRY_ROWS, GDN_CONV_DIM), BF16),
        ],
        scratch_shapes=[
            pltpu.VMEM((bb, CARRY_ROWS + c, GDN_CONV_DIM), BF16),
            pltpu.VMEM((2, bb * c, GDN_P_WIDTH), BF16),
            pltpu.VMEM((2, bb * c, BA_WIDTH), F32),
        ],
        compiler_params=_params(2, VMEM_LIMIT_BIG_TILE_BYTES),
        name="gdn_mixer",
    )(x3, norm_w, w_gdn, w_ba, conv_w, alog_row, dtb_row, gnw_row, conv0, s0, masks, shift)


def _rope_kernel(inv_ref, cos_ref, sin_ref, *, rows, offset):
    pos = lax.broadcasted_iota(jnp.int32, (rows, LANES), 0) + (pl.program_id(0) * rows + offset)
    ang = pos.astype(F32) * inv_ref[...]
    cos_ref[...] = jnp.cos(ang)
    sin_ref[...] = jnp.sin(ang)


def _rope_tables(inv_row, seq, offset):
    rows = min(512, seq)
    return pl.pallas_call(
        functools.partial(_rope_kernel, rows=rows, offset=offset),
        grid=(seq // rows,),
        in_specs=[pl.BlockSpec((1, LANES), lambda i: (0, 0))],
        out_specs=[pl.BlockSpec((rows, LANES), lambda i: (i, 0))] * 2,
        out_shape=[jax.ShapeDtypeStruct((seq, LANES), F32)] * 2,
        compiler_params=_params(1),
        name="rope_tables",
    )(inv_row)


RET_CHUNK = 256
RET_SEQS_PER_STEP = 2


def _pair_split_matrix():
    perm = np.concatenate([np.arange(0, RET_DK, 2), np.arange(1, RET_DK, 2)])
    return np.eye(RET_DK, dtype=np.float32)[:, perm]


def _ret_kernel(q_ref, k_ref, v_ref, g_ref, cos_ref, sin_ref, gnw_ref, s0_ref, split_ref, split_t_ref,
                o_ref, s_ref, dmat_ref, scale_ref, *, c, bb, n_blocks):
    first_block = pl.program_id(1) == 0
    probs = [(i, h) for i in range(bb) for h in range(RET_HEADS)]

    @pl.when(first_block)
    def _():
        for i, h in probs:
            s_ref[i, h] = _select_rows_exact(split_t_ref[...], s0_ref[i, h])

    @pl.when(jnp.logical_and(pl.program_id(0) == 0, first_block))
    def _():
        r = lax.broadcasted_iota(jnp.int32, (c, c), 0)
        q_i = lax.broadcasted_iota(jnp.int32, (c, c), 1)
        causal = r >= q_i
        dpos = (r - q_i).astype(F32)
        tok = lax.broadcasted_iota(jnp.int32, (c, LANES), 0).astype(F32)
        for h in range(RET_HEADS):
            lg = math.log(1.0 - 2.0 ** (-5.0 - h))
            dmat = jnp.where(causal, jnp.exp(jnp.where(causal, dpos * lg, 0.0)), 0.0)
            dmat_ref[h] = dmat * (RET_DK ** -0.5)
            scale_ref[0, h] = jnp.exp((tok + 1.0) * lg)
            scale_ref[1, h] = jnp.exp((c - 1.0 - tok) * lg) * (RET_DK ** -0.5)

    cos = cos_ref[...]
    sin = sin_ref[...]
    half = RET_DK // 2

    def rotate(x):
        xe, xo = x[:, 0:half], x[:, half:RET_DK]
        return jnp.concatenate([xe * cos - xo * sin, xo * cos + xe * sin], axis=1)

    both = lambda t: jnp.concatenate([t, t], axis=1)
    lanes = lambda h: slice(h * RET_DK, (h + 1) * RET_DK)
    qr = [rotate(q_ref[i, :, lanes(h)].astype(F32)) for i, h in probs]
    kr = [rotate(k_ref[i, :, lanes(h)].astype(F32)) for i, h in probs]
    attn = [_bdot_nt(qr[j], kr[j]) * dmat_ref[h] for j, (i, h) in enumerate(probs)]
    o = [_bdot(qr[j] * both(scale_ref[0, h]), s_ref[i, h]) + _bdot(attn[j], v_ref[i, :, lanes(h)])
         for j, (i, h) in enumerate(probs)]
    ds = [_bdot_tn(kr[j] * both(scale_ref[1, h]), v_ref[i, :, lanes(h)]) for j, (i, h) in enumerate(probs)]
    for j, (i, h) in enumerate(probs):
        s_ref[i, h] = s_ref[i, h] * math.exp(c * math.log(1.0 - 2.0 ** (-5.0 - h))) + ds[j]
        mu = jnp.mean(o[j], axis=-1, keepdims=True)
        dev = o[j] - mu
        var = jnp.mean(jnp.square(dev), axis=-1, keepdims=True)
        out = dev * lax.rsqrt(var + GN_EPS) * gnw_ref[:, lanes(h)]
        out = out * _silu(g_ref[i, :, lanes(h)].astype(F32))
        o_ref[i, :, lanes(h)] = out.astype(BF16)

    @pl.when(pl.program_id(1) == n_blocks - 1)
    def _():
        for i, h in probs:
            s_ref[i, h] = _select_rows_exact(split_ref[...], s_ref[i, h])


def _ret_mixer(p3, cos, sin, gnw_row, s0):
    batch, seq, _ = p3.shape
    c = min(RET_CHUNK, seq)
    bb = min(RET_SEQS_PER_STEP, batch)
    split = _pair_split_matrix()
    col = lambda j: pl.BlockSpec((bb, c, RET_QK), lambda b, s: (b, s, j))
    state_spec = pl.BlockSpec((bb, RET_HEADS, RET_DK, RET_DV), lambda b, s: (b, 0, 0, 0))
    square = pl.BlockSpec((RET_DK, RET_DK), lambda b, s: (0, 0))
    return pl.pallas_call(
        functools.partial(_ret_kernel, c=c, bb=bb, n_blocks=seq // c),
        grid=(batch // bb, seq // c),
        in_specs=[
            col(0), col(1), col(2), col(3),
            pl.BlockSpec((c, LANES), lambda b, s: (s, 0)),
            pl.BlockSpec((c, LANES), lambda b, s: (s, 0)),
            pl.BlockSpec((1, RET_V), lambda b, s: (0, 0)),
            state_spec,
            square,
            square,
        ],
        out_specs=[
            pl.BlockSpec((bb, c, RET_V), lambda b, s: (b, s, 0)),
            state_spec,
        ],
        out_shape=[
            jax.ShapeDtypeStruct((batch, seq, RET_V), BF16),
            jax.ShapeDtypeStruct((batch, RET_HEADS, RET_DK, RET_DV), F32),
        ],
        scratch_shapes=[pltpu.VMEM((RET_HEADS, c, c), F32), pltpu.VMEM((2, RET_HEADS, c, LANES), F32)],
        compiler_params=_params(2),
        name="ret_mixer",
    )(p3, p3, p3, p3, cos, sin, gnw_row, s0, jnp.asarray(split, dtype=BF16), jnp.asarray(split.T, dtype=BF16))


def _out_proj_kernel(x_ref, ma_ref, mr_ref, woa_ref, wor_ref, nw_ref, x1_ref, h2_ref):
    x1 = x_ref[...] + jnp.dot(ma_ref[...], woa_ref[...], preferred_element_type=F32)
    x1 = x1 + jnp.dot(mr_ref[...], wor_ref[...], preferred_element_type=F32)
    x1_ref[...] = x1
    h2_ref[...] = (x1 * _rms_scale(x1) * nw_ref[...]).astype(BF16)


def _out_proj(x2d, mix_a, mix_r, w_out_a, w_out_r, norm_w):
    t, d = x2d.shape
    tm = min(OUT_PROJ_TM, t)
    return pl.pallas_call(
        _out_proj_kernel,
        grid=(t // tm,),
        in_specs=[
            pl.BlockSpec((tm, d), lambda i: (i, 0)),
            pl.BlockSpec((tm, GDN_V), lambda i: (i, 0)),
            pl.BlockSpec((tm, RET_V), lambda i: (i, 0)),
            pl.BlockSpec((GDN_V, d), lambda i: (0, 0)),
            pl.BlockSpec((RET_V, d), lambda i: (0, 0)),
            pl.BlockSpec((1, d), lambda i: (0, 0)),
        ],
        out_specs=[
            pl.BlockSpec((tm, d), lambda i: (i, 0)),
            pl.BlockSpec((tm, d), lambda i: (i, 0)),
        ],
        out_shape=[
            jax.ShapeDtypeStruct((t, d), F32),
            jax.ShapeDtypeStruct((t, d), BF16),
        ],
        compiler_params=_params(1),
        name="out_proj",
    )(x2d, mix_a, mix_r, w_out_a, w_out_r, norm_w)


def _ffn_kernel(h2_ref, x1_ref, wg_ref, wu_ref, wd_ref, fw_ref, y_ref, *, n_ff, final_norm):
    f = pl.program_id(1)

    @pl.when(f == 0)
    def _():
        y_ref[...] = x1_ref[...]

    h = h2_ref[...]
    gate = jnp.dot(h, wg_ref[...], preferred_element_type=F32)
    up = jnp.dot(h, wu_ref[...], preferred_element_type=F32)
    act = (_silu(gate) * up).astype(BF16)
    y_ref[...] += jnp.dot(act, wd_ref[...], preferred_element_type=F32)

    if final_norm:
        @pl.when(f == n_ff - 1)
        def _():
            x2 = y_ref[...]
            y_ref[...] = x2 * _rms_scale(x2) * fw_ref[...]


def _ffn(h2, x1, w_gate_up, w_down, final_w, final_norm):
    t, d = x1.shape
    d_ff = w_down.shape[0]
    tm = min(FFN_TM, t)
    tf = FFN_TF if tm == FFN_TM else FFN_TF_FEW_ROWS
    n_ff = d_ff // tf
    return pl.pallas_call(
        functools.partial(_ffn_kernel, n_ff=n_ff, final_norm=final_norm),
        grid=(t // tm, n_ff),
        in_specs=[
            pl.BlockSpec((tm, d), lambda i, f: (i, 0)),
            pl.BlockSpec((tm, d), lambda i, f: (i, 0)),
            pl.BlockSpec((d, tf), lambda i, f: (0, f)),
            pl.BlockSpec((d, tf), lambda i, f: (0, n_ff + f)),
            pl.BlockSpec((tf, d), lambda i, f: (f, 0)),
            pl.BlockSpec((1, d), lambda i, f: (0, 0)),
        ],
        out_specs=pl.BlockSpec((tm, d), lambda i, f: (i, 0)),
        out_shape=jax.ShapeDtypeStruct((t, d), F32),
        compiler_params=_params(2, VMEM_LIMIT_BIG_TILE_BYTES),
        name="ffn",
    )(h2, x1, w_gate_up, w_gate_up, w_down, final_w)


W_IN_PREP_ROWS = 256


def _w_in_layout_kernel(w_ref, split_ref, main_ref, ba_ref):
    o_b = GDN_CONV_DIM + GDN_V
    o_qr = o_b + 2 * GDN_HEADS
    main_ref[:, 0:o_b] = w_ref[:, 0:o_b]
    for head in range(2 * RET_HEADS):
        src = w_ref[:, o_qr + head * RET_DK:o_qr + (head + 1) * RET_DK]
        main_ref[:, o_b + head * RET_DK:o_b + (head + 1) * RET_DK] = jnp.dot(
            src, split_ref[...], preferred_element_type=F32).astype(BF16)
    rest = o_b + 2 * RET_QK
    main_ref[:, rest:P_WIDTH] = w_ref[:, o_qr + 2 * RET_QK:o_qr + 2 * RET_QK + P_WIDTH - rest]
    gates = w_ref[:, o_b:o_qr]
    pad = jnp.zeros((gates.shape[0], LANES - GDN_HEADS), BF16)
    ba_ref[...] = jnp.concatenate([gates[:, 0:GDN_HEADS], pad, gates[:, GDN_HEADS:2 * GDN_HEADS], pad], axis=1)


def _w_in_layout(w_in_bf16):
    d, width = w_in_bf16.shape
    rows = min(W_IN_PREP_ROWS, d)
    return pl.pallas_call(
        _w_in_layout_kernel,
        grid=(d // rows,),
        in_specs=[pl.BlockSpec((rows, width), lambda i: (i, 0)),
                  pl.BlockSpec((RET_DK, RET_DK), lambda i: (0, 0))],
        out_specs=[pl.BlockSpec((rows, P_WIDTH), lambda i: (i, 0)),
                   pl.BlockSpec((rows, BA_WIDTH), lambda i: (i, 0))],
        out_shape=[jax.ShapeDtypeStruct((d, P_WIDTH), BF16), jax.ShapeDtypeStruct((d, BA_WIDTH), BF16)],
        compiler_params=_params(1),
        name="w_in_layout",
    )(w_in_bf16, jnp.asarray(_pair_split_matrix(), dtype=BF16))


def _prep_layer_weights(w_in, w_out, w_gate_up, w_down, a_log, dt_bias, gdn_norm_w, ret_gn_w,
                        attn_norm_w, ffn_norm_w, conv_w):
    d = w_in.shape[0]
    w_main, w_ba = _w_in_layout(w_in.astype(BF16))
    row_pad = lambda v: jnp.pad(v.astype(F32), (0, LANES - v.shape[0])).reshape(1, LANES)
    return dict(
        w_gdn=w_main[:, 0:GDN_P_WIDTH], w_ret=w_main[:, GDN_P_WIDTH:], w_ba=w_ba,
        w_out_a=w_out[0:GDN_V].astype(BF16), w_out_r=w_out[GDN_V:].astype(BF16),
        w_gate_up=w_gate_up.astype(BF16), w_down=w_down.astype(BF16),
        alog_row=row_pad(a_log), dtb_row=row_pad(dt_bias),
        gnw_row=gdn_norm_w.reshape(1, GDN_DV).astype(F32),
        ret_gnw_row=ret_gn_w.reshape(1, RET_V).astype(F32),
        attn_norm_w=attn_norm_w.reshape(1, d), ffn_norm_w=ffn_norm_w.reshape(1, d),
        conv_w=conv_w,
    )


def _layer(x2d, batch, seq, cos, sin, conv_state, s_gdn, s_ret, lw, final_w, final_norm):
    d = x2d.shape[1]
    conv0 = jnp.pad(conv_state.astype(BF16), ((0, 0), (CARRY_ROWS - (CONV_W - 1), 0), (0, 0)))
    mix_a, s_gdn_new, tail = _gdn_mixer(x2d.reshape(batch, seq, d), lw["attn_norm_w"], lw["w_gdn"], lw["w_ba"],
                                        lw["conv_w"], lw["alog_row"], lw["dtb_row"], lw["gnw_row"],
                                        conv0, s_gdn.astype(F32))
    p_ret = _in_proj(x2d, lw["attn_norm_w"], lw["w_ret"]).reshape(batch, seq, P_WIDTH - GDN_P_WIDTH)
    mix_r, s_ret_new = _ret_mixer(p_ret, cos, sin, lw["ret_gnw_row"], s_ret.astype(F32))
    mix_a = mix_a.reshape(batch * seq, GDN_V)
    mix_r = mix_r.reshape(batch * seq, RET_V)
    x1, h2 = _out_proj(x2d, mix_a, mix_r, lw["w_out_a"], lw["w_out_r"], lw["ffn_norm_w"])
    y = _ffn(h2, x1, lw["w_gate_up"], lw["w_down"], final_w, final_norm)
    rows = tail[:, CARRY_ROWS - (CONV_W - 1):, :].astype(conv_state.dtype)
    return y, rows, s_gdn_new.astype(s_gdn.dtype), s_ret_new.astype(s_ret.dtype)


def kernel(x_prompt, x_sample, state_gdn_conv, state_gdn, state_ret, attn_norm_w, w_in, conv_w, a_log, dt_bias, gdn_norm_w, ret_gn_w, w_out, ffn_norm_w, w_gate_up, w_down, final_norm_w):
    depth = w_in.shape[0]
    bp, lp, d = x_prompt.shape
    bs, ls, _ = x_sample.shape
    assert lp >= CONV_W - 1 and ls >= CONV_W - 1
    inv = 1.0 / (ROPE_BASE ** jnp.linspace(0.0, 1.0, RET_DK // 2, dtype=F32))
    inv_row = inv.reshape(1, LANES)
    cos_p, sin_p = _rope_tables(inv_row, lp, 0)
    cos_s, sin_s = _rope_tables(inv_row, ls, PAST_LEN)
    final_w = final_norm_w.reshape(1, d)
    hp = x_prompt.reshape(bp * lp, d)
    hs = x_sample.reshape(bs * ls, d)
    outs = [[] for _ in range(6)]
    for l in range(depth):
        lw = _prep_layer_weights(w_in[l], w_out[l], w_gate_up[l], w_down[l], a_log[l], dt_bias[l],
                                 gdn_norm_w[l], ret_gn_w[l], attn_norm_w[l], ffn_norm_w[l], conv_w[l])
        last = l == depth - 1
        conv0 = jnp.zeros((bp, CONV_W - 1, GDN_CONV_DIM), x_prompt.dtype)
        sg0 = jnp.zeros((bp, GDN_HEADS, GDN_DK, GDN_DV), state_gdn.dtype)
        sr0 = jnp.zeros((bp, RET_HEADS, RET_DK, RET_DV), state_ret.dtype)
        hp, c_p, g_p, r_p = _layer(hp, bp, lp, cos_p, sin_p, conv0, sg0, sr0, lw, final_w, last)
        hs, c_s, g_s, r_s = _layer(hs, bs, ls, cos_s, sin_s, state_gdn_conv[l], state_gdn[l],
                                   state_ret[l], lw, final_w, last)
        for acc, val in zip(outs, (c_p, g_p, r_p, c_s, g_s, r_s)):
            acc.append(val)
    return (hp.reshape(bp, lp, d), hs.reshape(bs, ls, d)) + tuple(jnp.stack(o) for o in outs)
```

```python
import functools
import math

import jax
import jax.numpy as jnp
import numpy as np
from jax import lax
from jax.experimental import pallas as pl
from jax.experimental.pallas import tpu as pltpu

F32 = jnp.float32
BF16 = jnp.bfloat16

GDN_HEADS = 8
GDN_DK = 128
GDN_DV = 128
CONV_W = 4
RET_HEADS = 4
RET_DK = 256
RET_DV = 256
GDN_QK = GDN_HEADS * GDN_DK
GDN_V = GDN_HEADS * GDN_DV
GDN_CONV_DIM = 2 * GDN_QK + GDN_V
RET_QK = RET_HEADS * RET_DK
RET_V = RET_HEADS * RET_DV
CHUNK = 64
PAST_LEN = 4096
RMS_EPS = 1e-6
GN_EPS = 1e-5
L2_EPS = 1e-6
ROPE_BASE = 10000.0

LANES = 128
SUBLANES = 8
V7X_VMEM_BYTES = 64 * 1024 * 1024
VMEM_LIMIT_BYTES = 52 * 1024 * 1024
VMEM_LIMIT_BIG_TILE_BYTES = 60 * 1024 * 1024
TRI_BLOCK = 16

P_WIDTH = 8 * 1024
BA_WIDTH = 2 * LANES

OUT_PROJ_TM = 512
FFN_TM, FFN_TF = 1024, 512
FFN_TF_FEW_ROWS = 1408


def _bdot(a, b):
    return jnp.dot(a.astype(BF16), b.astype(BF16), preferred_element_type=F32)


def _bdot_nt(a, b):
    return lax.dot_general(a.astype(BF16), b.astype(BF16), (((1,), (1,)), ((), ())),
                           preferred_element_type=F32)


def _bdot_tn(a, b):
    return lax.dot_general(a.astype(BF16), b.astype(BF16), (((0,), (0,)), ((), ())),
                           preferred_element_type=F32)


def _select_rows_exact(sel, x):
    hi = x.astype(BF16)
    rest = x - hi.astype(F32)
    mid = rest.astype(BF16)
    lo = (rest - mid.astype(F32)).astype(BF16)
    n = x.shape[1]
    parts = jnp.dot(sel, jnp.concatenate([hi, mid, lo], axis=1), preferred_element_type=F32)
    return parts[:, 0:n] + parts[:, n:2 * n] + parts[:, 2 * n:3 * n]


def _sigmoid(x):
    return 1.0 / (1.0 + jnp.exp(-x))


def _silu(x):
    return x * _sigmoid(x)


def _rms_scale(x):
    return lax.rsqrt(jnp.mean(x * x, axis=-1, keepdims=True) + RMS_EPS)


def _params(n_grid_dims, vmem_limit_bytes=VMEM_LIMIT_BYTES):
    return pltpu.CompilerParams(dimension_semantics=("arbitrary",) * n_grid_dims,
                                vmem_limit_bytes=vmem_limit_bytes)


M_CAUSAL, M_STRICT, M_EYE, M_DIAG, M_MERGE0 = 0, 1, 2, 3, 4
CARRY_ROWS = 16
GDN_SEQS_PER_STEP = 4


def _gdn_constants(c):
    r = np.arange(c)[:, None]
    q = np.arange(c)[None, :]
    masks = [r >= q, r > q, r == q, (r // TRI_BLOCK) == (q // TRI_BLOCK)]
    blk = TRI_BLOCK
    while blk < c:
        masks.append(((r // (2 * blk)) == (q // (2 * blk))) & ((r // blk) != (q // blk)))
        blk *= 2
    masks = np.stack(masks).astype(np.float32)
    shift = np.zeros(((CONV_W - 1) * c, CARRY_ROWS + c), np.float32)
    for k in range(CONV_W - 1):
        shift[k * c + np.arange(c), CARRY_ROWS + np.arange(c) - (CONV_W - 1) + k] = 1.0
    return jnp.asarray(masks), jnp.asarray(shift, dtype=BF16)


def _tri_inv(a_list, masks_ref, c):
    pw = [a * masks_ref[M_DIAG] for a in a_list]
    p = [masks_ref[M_EYE] - d for d in pw]
    for _ in range(int(math.log2(TRI_BLOCK)) - 1):
        pw = [_bdot(x, x) for x in pw]
        p = [x + _bdot(x, y) for x, y in zip(p, pw)]
    blk, level = TRI_BLOCK, M_MERGE0
    while blk < c:
        t = [_bdot(x, a * masks_ref[level]) for x, a in zip(p, a_list)]
        p = [x - _bdot(y, x) for x, y in zip(p, t)]
        blk, level = 2 * blk, level + 1
    return p


GDN_P_WIDTH = GDN_CONV_DIM + GDN_V
GDN_PROJ_TILE = 512


def _gdn_kernel(x_ref, nw_ref, w_ref, wba_ref, convw_ref, alog_ref, dtb_ref, gnw_ref, conv0_ref, s0_ref,
                masks_ref, shift_ref, o_ref, s_ref, tail_ref, xbuf_ref, p_buf, ba_buf, *, c, bb):
    t = pl.program_id(1)
    nslot = t % 2
    pslot = 1 - nslot

    @pl.when(t == 0)
    def _():
        p_buf[1] = jnp.zeros(p_buf.shape[1:], BF16)
        ba_buf[1] = jnp.zeros(ba_buf.shape[1:], F32)
        xbuf_ref[...] = jnp.zeros(xbuf_ref.shape, BF16)
        s_ref[...] = s0_ref[...]

    @pl.when(t == 1)
    def _():
        xbuf_ref[:, 0:CARRY_ROWS, :] = conv0_ref[...]
        s_ref[...] = s0_ref[...]

    heads = range(GDN_HEADS)
    seqs = range(bb)
    col = lambda arr, h: arr[:, h:h + 1]
    lanes = lambda base, h: slice(base + h * GDN_DK, base + (h + 1) * GDN_DK)
    rows = lambda arr, h: arr[h * c:(h + 1) * c, :]
    seq_rows = lambda i: slice(i * c, (i + 1) * c)

    x = x_ref[...].reshape(bb * c, x_ref.shape[-1])
    h_next = (x * _rms_scale(x) * nw_ref[...]).astype(BF16)
    proj_tiles = [slice(lo, lo + GDN_PROJ_TILE) for lo in range(0, GDN_P_WIDTH, GDN_PROJ_TILE)]

    def project(tiles):
        for cols in tiles:
            p_buf[nslot, :, cols] = jnp.dot(h_next, w_ref[:, cols], preferred_element_type=F32).astype(BF16)

    ba_buf[nslot] = jnp.dot(h_next, wba_ref[...], preferred_element_type=F32)

    qkv = lambda i: p_buf[pslot, seq_rows(i), 0:GDN_CONV_DIM]

    def shifted_rows(i):
        xbuf_ref[i, CARRY_ROWS:CARRY_ROWS + c, :] = qkv(i)
        shifted = jnp.dot(shift_ref[...], xbuf_ref[i], preferred_element_type=F32)
        xbuf_ref[i, 0:CARRY_ROWS, :] = xbuf_ref[i, c:c + CARRY_ROWS, :]
        return shifted

    def gates(i):
        ba = ba_buf[pslot, seq_rows(i), :]
        beta = _sigmoid(ba[:, 0:LANES])
        sp_in = ba[:, LANES:2 * LANES] + dtb_ref[...]
        softplus = jnp.maximum(sp_in, 0.0) + jnp.log(1.0 + jnp.exp(-jnp.abs(sp_in)))
        gc = _select_rows_exact(masks_ref[M_CAUSAL].astype(BF16), -jnp.exp(alog_ref[...]) * softplus)
        gc_t = jnp.concatenate([gc, jnp.zeros((LANES - c, LANES), F32)], axis=0).T
        g_last = gc[c - 1:c, :]
        e_gc = jnp.exp(gc)
        return dict(beta=beta, gc=gc, gc_t=gc_t, e_gc=e_gc, e_kd=jnp.exp(g_last - gc),
                    e_last=jnp.exp(g_last), beta_e=beta * e_gc)

    def conv_taps(i, shifted):
        w = convw_ref[...]
        y = shifted[0:c, :] * w[0:1, :]
        y = y + shifted[c:2 * c, :] * w[1:2, :]
        y = y + shifted[2 * c:3 * c, :] * w[2:3, :]
        y = y + qkv(i).astype(F32) * w[3:4, :]
        return _silu(y)

    def inv_norm(y, base, fill):
        sq = jnp.concatenate([jnp.square(y[:, lanes(base, h)]) for h in heads], axis=0)
        tot = _bdot(sq, jnp.full((GDN_DK, LANES), fill, F32))
        return lax.rsqrt(tot + fill * L2_EPS)

    n_tiles = len(proj_tiles)
    shifted = [shifted_rows(i) for i in seqs]
    project(proj_tiles[0:n_tiles // 2])
    pre = [gates(i) for i in seqs]
    y = [conv_taps(i, shifted[i]) for i in seqs]
    project(proj_tiles[n_tiles // 2:])
    inv_q = [inv_norm(y[i], 0, float(GDN_DK)) for i in seqs]
    inv_k = [inv_norm(y[i], GDN_QK, 1.0) for i in seqs]
    probs = [(i, h) for i in seqs for h in heads]
    qn = [y[i][:, lanes(0, h)] * rows(inv_q[i], h) for i, h in probs]
    kn = [y[i][:, lanes(GDN_QK, h)] * rows(inv_k[i], h) for i, h in probs]
    kb = [kn[j] * col(pre[i]["beta"], h) for j, (i, h) in enumerate(probs)]
    qk = [_bdot_nt(jnp.concatenate([qn[j], kb[j]], axis=0), kn[j]) for j in range(len(probs))]
    attn, a_mat = [], []
    for j, (i, h) in enumerate(probs):
        diff = col(pre[i]["gc"], h) - pre[i]["gc_t"][h:h + 1, 0:c]
        decay = jnp.exp(jnp.minimum(diff, 0.0))
        attn.append(qk[j][0:c, :] * (decay * masks_ref[M_CAUSAL]))
        a_mat.append(qk[j][c:2 * c, :] * (decay * masks_ref[M_STRICT]))
    t_inv = _tri_inv(a_mat, masks_ref, c)
    w_u = [_bdot(t_inv[j], jnp.concatenate([y[i][:, lanes(2 * GDN_QK, h)] * col(pre[i]["beta"], h),
                                            kn[j] * col(pre[i]["beta_e"], h)], axis=1))
           for j, (i, h) in enumerate(probs)]
    x_s = [_bdot(jnp.concatenate([w_u[j][:, GDN_DV:2 * GDN_DV], qn[j] * col(pre[i]["e_gc"], h)], axis=0),
                 s_ref[i, h]) for j, (i, h) in enumerate(probs)]
    u = [w_u[j][:, 0:GDN_DV] - x_s[j][0:c, :] for j in range(len(probs))]
    o = [x_s[j][c:2 * c, :] + _bdot(attn[j], u[j]) for j in range(len(probs))]
    ds = [_bdot_tn(kn[j] * col(pre[i]["e_kd"], h), u[j]) for j, (i, h) in enumerate(probs)]
    gnw = gnw_ref[...]
    for j, (i, h) in enumerate(probs):
        s_ref[i, h] = s_ref[i, h] * col(pre[i]["e_last"], h) + ds[j]
        out = o[j] * _rms_scale(o[j]) * gnw
        z = p_buf[pslot, seq_rows(i), GDN_CONV_DIM + h * GDN_DV:GDN_CONV_DIM + (h + 1) * GDN_DV]
        out = out * _silu(z.astype(F32))
        o_ref[i, :, lanes(0, h)] = out.astype(BF16)
    tail_ref[...] = xbuf_ref[:, 0:CARRY_ROWS, :]


def _gdn_mixer(x3, norm_w, w_gdn, w_ba, conv_w, alog_row, dtb_row, gnw_row, conv0, s0):
    batch, seq, d = x3.shape
    c = min(CHUNK, seq)
    bb = min(GDN_SEQS_PER_STEP, batch)
    n = seq // c
    masks, shift = _gdn_constants(c)
    const = lambda shape: pl.BlockSpec(shape, lambda b, t: (0,) * len(shape))
    state_spec = pl.BlockSpec((bb, GDN_HEADS, GDN_DK, GDN_DV), lambda b, t: (b, 0, 0, 0))
    carry_spec = pl.BlockSpec((bb, CARRY_ROWS, GDN_CONV_DIM), lambda b, t: (b, 0, 0))
    return pl.pallas_call(
        functools.partial(_gdn_kernel, c=c, bb=bb),
        grid=(batch // bb, n + 1),
        in_specs=[
            pl.BlockSpec((bb, c, d), lambda b, t: (b, jnp.minimum(t, n - 1), 0)),
            const((1, d)),
            const(w_gdn.shape),
            const(w_ba.shape),
            const((CONV_W, GDN_CONV_DIM)),
            const((1, LANES)),
            const((1, LANES)),
            const((1, GDN_DV)),
            carry_spec,
            state_spec,
            const(masks.shape),
            const(shift.shape),
        ],
        out_specs=[
            pl.BlockSpec((bb, c, GDN_V), lambda b, t: (b, jnp.maximum(t - 1, 0), 0)),
            state_spec,
            carry_spec,
        ],
        out_shape=[
            jax.ShapeDtypeStruct((batch, seq, GDN_V), BF16),
            jax.ShapeDtypeStruct((batch, GDN_HEADS, GDN_DK, GDN_DV), F32),
            jax.ShapeDtypeStruct((batch, CARRY_ROWS, GDN_CONV_DIM), BF16),
        ],
        scratch_shapes=[
            pltpu.VMEM((bb, CARRY_ROWS + c, GDN_CONV_DIM), BF16),
            pltpu.VMEM((2, bb * c, GDN_P_WIDTH), BF16),
            pltpu.VMEM((2, bb * c, BA_WIDTH), F32),
        ],
        compiler_params=_params(2, VMEM_LIMIT_BIG_TILE_BYTES),
        name="gdn_mixer",
    )(x3, norm_w, w_gdn, w_ba, conv_w, alog_row, dtb_row, gnw_row, conv0, s0, masks, shift)


def _rope_kernel(inv_ref, cos_ref, sin_ref, *, rows, offset):
    pos = lax.broadcasted_iota(jnp.int32, (rows, LANES), 0) + (pl.program_id(0) * rows + offset)
    ang = pos.astype(F32) * inv_ref[...]
    cos_ref[...] = jnp.cos(ang)
    sin_ref[...] = jnp.sin(ang)


def _rope_tables(inv_row, seq, offset):
    rows = min(512, seq)
    return pl.pallas_call(
        functools.partial(_rope_kernel, rows=rows, offset=offset),
        grid=(seq // rows,),
        in_specs=[pl.BlockSpec((1, LANES), lambda i: (0, 0))],
        out_specs=[pl.BlockSpec((rows, LANES), lambda i: (i, 0))] * 2,
        out_shape=[jax.ShapeDtypeStruct((seq, LANES), F32)] * 2,
        compiler_params=_params(1),
        name="rope_tables",
    )(inv_row)


RET_CHUNK = 256
RET_SEQS_PER_STEP = 2


def _pair_split_matrix():
    perm = np.concatenate([np.arange(0, RET_DK, 2), np.arange(1, RET_DK, 2)])
    return np.eye(RET_DK, dtype=np.float32)[:, perm]


RET_P_WIDTH = 2 * RET_QK + 2 * RET_V
RET_PROJ_TILE = 512


def _ret_kernel(x_ref, nw_ref, w_ref, cos_ref, sin_ref, gnw_ref, s0_ref, split_ref, split_t_ref,
                o_ref, s_ref, dmat_ref, scale_ref, p_buf, *, c, bb, n_blocks):
    t = pl.program_id(1)
    nslot = t % 2
    pslot = 1 - nslot
    probs = [(i, h) for i in range(bb) for h in range(RET_HEADS)]

    def load_state():
        for i, h in probs:
            s_ref[i, h] = _select_rows_exact(split_t_ref[...], s0_ref[i, h])

    @pl.when(t == 0)
    def _():
        p_buf[1] = jnp.zeros(p_buf.shape[1:], BF16)
        load_state()

    pl.when(t == 1)(load_state)

    @pl.when(jnp.logical_and(pl.program_id(0) == 0, t == 0))
    def _():
        r = lax.broadcasted_iota(jnp.int32, (c, c), 0)
        q_i = lax.broadcasted_iota(jnp.int32, (c, c), 1)
        causal = r >= q_i
        dpos = (r - q_i).astype(F32)
        tok = lax.broadcasted_iota(jnp.int32, (c, LANES), 0).astype(F32)
        for h in range(RET_HEADS):
            lg = math.log(1.0 - 2.0 ** (-5.0 - h))
            dmat = jnp.where(causal, jnp.exp(jnp.where(causal, dpos * lg, 0.0)), 0.0)
            dmat_ref[h] = dmat * (RET_DK ** -0.5)
            scale_ref[0, h] = jnp.exp((tok + 1.0) * lg)
            scale_ref[1, h] = jnp.exp((c - 1.0 - tok) * lg) * (RET_DK ** -0.5)

    x = x_ref[...].reshape(bb * c, x_ref.shape[-1])
    h_next = (x * _rms_scale(x) * nw_ref[...]).astype(BF16)
    proj_tiles = [slice(lo, lo + RET_PROJ_TILE) for lo in range(0, RET_P_WIDTH, RET_PROJ_TILE)]

    def project(tiles):
        for cols in tiles:
            p_buf[nslot, :, cols] = jnp.dot(h_next, w_ref[:, cols], preferred_element_type=F32).astype(BF16)

    cos = cos_ref[...]
    sin = sin_ref[...]
    half = RET_DK // 2

    def rotate(x):
        xe, xo = x[:, 0:half], x[:, half:RET_DK]
        return jnp.concatenate([xe * cos - xo * sin, xo * cos + xe * sin], axis=1)

    both = lambda arr: jnp.concatenate([arr, arr], axis=1)
    prev = lambda g, i, h: p_buf[pslot, i * c:(i + 1) * c, g * RET_QK + h * RET_DK:g * RET_QK + (h + 1) * RET_DK]
    lanes = lambda h: slice(h * RET_DK, (h + 1) * RET_DK)
    n_tiles = len(proj_tiles)
    project(proj_tiles[0:n_tiles // 2])
    qr = [rotate(prev(0, i, h).astype(F32)) for i, h in probs]
    kr = [rotate(prev(1, i, h).astype(F32)) for i, h in probs]
    attn = [_bdot_nt(qr[j], kr[j]) * dmat_ref[h] for j, (i, h) in enumerate(probs)]
    o = [_bdot(qr[j] * both(scale_ref[0, h]), s_ref[i, h]) + _bdot(attn[j], prev(2, i, h))
         for j, (i, h) in enumerate(probs)]
    ds = [_bdot_tn(kr[j] * both(scale_ref[1, h]), prev(2, i, h)) for j, (i, h) in enumerate(probs)]
    project(proj_tiles[n_tiles // 2:])
    for j, (i, h) in enumerate(probs):
        s_ref[i, h] = s_ref[i, h] * math.exp(c * math.log(1.0 - 2.0 ** (-5.0 - h))) + ds[j]
        mu = jnp.mean(o[j], axis=-1, keepdims=True)
        dev = o[j] - mu
        var = jnp.mean(jnp.square(dev), axis=-1, keepdims=True)
        out = dev * lax.rsqrt(var + GN_EPS) * gnw_ref[:, lanes(h)]
        out = out * _silu(prev(3, i, h).astype(F32))
        o_ref[i, :, lanes(h)] = out.astype(BF16)

    @pl.when(t == n_blocks)
    def _():
        for i, h in probs:
            s_ref[i, h] = _select_rows_exact(split_ref[...], s_ref[i, h])


def _ret_mixer(x3, norm_w, w_ret, cos, sin, gnw_row, s0):
    batch, seq, d = x3.shape
    c = min(RET_CHUNK, seq)
    bb = min(RET_SEQS_PER_STEP, batch)
    n = seq // c
    split = _pair_split_matrix()
    const = lambda shape: pl.BlockSpec(shape, lambda b, t: (0,) * len(shape))
    state_spec = pl.BlockSpec((bb, RET_HEADS, RET_DK, RET_DV), lambda b, t: (b, 0, 0, 0))
    prev_rows = lambda b, t: (jnp.maximum(t - 1, 0), 0)
    return pl.pallas_call(
        functools.partial(_ret_kernel, c=c, bb=bb, n_blocks=n),
        grid=(batch // bb, n + 1),
        in_specs=[
            pl.BlockSpec((bb, c, d), lambda b, t: (b, jnp.minimum(t, n - 1), 0)),
            const((1, d)),
            const(w_ret.shape),
            pl.BlockSpec((c, LANES), prev_rows),
            pl.BlockSpec((c, LANES), prev_rows),
            const((1, RET_V)),
            state_spec,
            const((RET_DK, RET_DK)),
            const((RET_DK, RET_DK)),
        ],
        out_specs=[
            pl.BlockSpec((bb, c, RET_V), lambda b, t: (b, jnp.maximum(t - 1, 0), 0)),
            state_spec,
        ],
        out_shape=[
            jax.ShapeDtypeStruct((batch, seq, RET_V), BF16),
            jax.ShapeDtypeStruct((batch, RET_HEADS, RET_DK, RET_DV), F32),
        ],
        scratch_shapes=[pltpu.VMEM((RET_HEADS, c, c), F32), pltpu.VMEM((2, RET_HEADS, c, LANES), F32),
                        pltpu.VMEM((2, bb * c, RET_P_WIDTH), BF16)],
        compiler_params=_params(2, VMEM_LIMIT_BIG_TILE_BYTES),
        name="ret_mixer",
    )(x3, norm_w, w_ret, cos, sin, gnw_row, s0, jnp.asarray(split, dtype=BF16), jnp.asarray(split.T, dtype=BF16))


def _out_proj_kernel(x_ref, ma_ref, mr_ref, woa_ref, wor_ref, nw_ref, x1_ref, h2_ref):
    x1 = x_ref[...] + jnp.dot(ma_ref[...], woa_ref[...], preferred_element_type=F32)
    x1 = x1 + jnp.dot(mr_ref[...], wor_ref[...], preferred_element_type=F32)
    x1_ref[...] = x1
    h2_ref[...] = (x1 * _rms_scale(x1) * nw_ref[...]).astype(BF16)


def _out_proj(x2d, mix_a, mix_r, w_out_a, w_out_r, norm_w):
    t, d = x2d.shape
    tm = min(OUT_PROJ_TM, t)
    return pl.pallas_call(
        _out_proj_kernel,
        grid=(t // tm,),
        in_specs=[
            pl.BlockSpec((tm, d), lambda i: (i, 0)),
            pl.BlockSpec((tm, GDN_V), lambda i: (i, 0)),
            pl.BlockSpec((tm, RET_V), lambda i: (i, 0)),
            pl.BlockSpec((GDN_V, d), lambda i: (0, 0)),
            pl.BlockSpec((RET_V, d), lambda i: (0, 0)),
            pl.BlockSpec((1, d), lambda i: (0, 0)),
        ],
        out_specs=[
            pl.BlockSpec((tm, d), lambda i: (i, 0)),
            pl.BlockSpec((tm, d), lambda i: (i, 0)),
        ],
        out_shape=[
            jax.ShapeDtypeStruct((t, d), F32),
            jax.ShapeDtypeStruct((t, d), BF16),
        ],
        compiler_params=_params(1),
        name="out_proj",
    )(x2d, mix_a, mix_r, w_out_a, w_out_r, norm_w)


def _ffn_kernel(h2_ref, x1_ref, wg_ref, wu_ref, wd_ref, fw_ref, y_ref, *, n_ff, final_norm):
    f = pl.program_id(1)

    @pl.when(f == 0)
    def _():
        y_ref[...] = x1_ref[...]

    h = h2_ref[...]
    gate = jnp.dot(h, wg_ref[...], preferred_element_type=F32)
    up = jnp.dot(h, wu_ref[...], preferred_element_type=F32)
    act = (_silu(gate) * up).astype(BF16)
    y_ref[...] += jnp.dot(act, wd_ref[...], preferred_element_type=F32)

    if final_norm:
        @pl.when(f == n_ff - 1)
        def _():
            x2 = y_ref[...]
            y_ref[...] = x2 * _rms_scale(x2) * fw_ref[...]


def _ffn(h2, x1, w_gate_up, w_down, final_w, final_norm):
    t, d = x1.shape
    d_ff = w_down.shape[0]
    tm = min(FFN_TM, t)
    tf = FFN_TF if tm == FFN_TM else FFN_TF_FEW_ROWS
    n_ff = d_ff // tf
    return pl.pallas_call(
        functools.partial(_ffn_kernel, n_ff=n_ff, final_norm=final_norm),
        grid=(t // tm, n_ff),
        in_specs=[
            pl.BlockSpec((tm, d), lambda i, f: (i, 0)),
            pl.BlockSpec((tm, d), lambda i, f: (i, 0)),
            pl.BlockSpec((d, tf), lambda i, f: (0, f)),
            pl.BlockSpec((d, tf), lambda i, f: (0, n_ff + f)),
            pl.BlockSpec((tf, d), lambda i, f: (f, 0)),
            pl.BlockSpec((1, d), lambda i, f: (0, 0)),
        ],
        out_specs=pl.BlockSpec((tm, d), lambda i, f: (i, 0)),
        out_shape=jax.ShapeDtypeStruct((t, d), F32),
        compiler_params=_params(2, VMEM_LIMIT_BIG_TILE_BYTES),
        name="ffn",
    )(h2, x1, w_gate_up, w_gate_up, w_down, final_w)


W_IN_PREP_ROWS = 256


def _w_in_layout_kernel(w_ref, split_ref, gdn_ref, ret_ref, ba_ref):
    o_b = GDN_CONV_DIM + GDN_V
    o_qr = o_b + 2 * GDN_HEADS
    gdn_ref[...] = w_ref[:, 0:o_b]
    for head in range(2 * RET_HEADS):
        src = w_ref[:, o_qr + head * RET_DK:o_qr + (head + 1) * RET_DK]
        ret_ref[:, head * RET_DK:(head + 1) * RET_DK] = jnp.dot(
            src, split_ref[...], preferred_element_type=F32).astype(BF16)
    ret_ref[:, 2 * RET_QK:RET_P_WIDTH] = w_ref[:, o_qr + 2 * RET_QK:o_qr + RET_P_WIDTH]
    gates = w_ref[:, o_b:o_qr]
    pad = jnp.zeros((gates.shape[0], LANES - GDN_HEADS), BF16)
    ba_ref[...] = jnp.concatenate([gates[:, 0:GDN_HEADS], pad, gates[:, GDN_HEADS:2 * GDN_HEADS], pad], axis=1)


def _w_in_layout(w_in_bf16):
    d, width = w_in_bf16.shape
    rows = min(W_IN_PREP_ROWS, d)
    out = lambda n: pl.BlockSpec((rows, n), lambda i: (i, 0))
    return pl.pallas_call(
        _w_in_layout_kernel,
        grid=(d // rows,),
        in_specs=[pl.BlockSpec((rows, width), lambda i: (i, 0)),
                  pl.BlockSpec((RET_DK, RET_DK), lambda i: (0, 0))],
        out_specs=[out(GDN_P_WIDTH), out(RET_P_WIDTH), out(BA_WIDTH)],
        out_shape=[jax.ShapeDtypeStruct((d, n), BF16) for n in (GDN_P_WIDTH, RET_P_WIDTH, BA_WIDTH)],
        compiler_params=_params(1),
        name="w_in_layout",
    )(w_in_bf16, jnp.asarray(_pair_split_matrix(), dtype=BF16))


def _prep_layer_weights(w_in, w_out, w_gate_up, w_down, a_log, dt_bias, gdn_norm_w, ret_gn_w,
                        attn_norm_w, ffn_norm_w, conv_w):
    d = w_in.shape[0]
    w_gdn, w_ret, w_ba = _w_in_layout(w_in.astype(BF16))
    row_pad = lambda v: jnp.pad(v.astype(F32), (0, LANES - v.shape[0])).reshape(1, LANES)
    return dict(
        w_gdn=w_gdn, w_ret=w_ret, w_ba=w_ba,
        w_out_a=w_out[0:GDN_V].astype(BF16), w_out_r=w_out[GDN_V:].astype(BF16),
        w_gate_up=w_gate_up.astype(BF16), w_down=w_down.astype(BF16),
        alog_row=row_pad(a_log), dtb_row=row_pad(dt_bias),
        gnw_row=gdn_norm_w.reshape(1, GDN_DV).astype(F32),
        ret_gnw_row=ret_gn_w.reshape(1, RET_V).astype(F32),
        attn_norm_w=attn_norm_w.reshape(1, d), ffn_norm_w=ffn_norm_w.reshape(1, d),
        conv_w=conv_w,
    )


def _layer(x2d, batch, seq, cos, sin, conv_state, s_gdn, s_ret, lw, final_w, final_norm):
    d = x2d.shape[1]
    conv0 = jnp.pad(conv_state.astype(BF16), ((0, 0), (CARRY_ROWS - (CONV_W - 1), 0), (0, 0)))
    mix_a, s_gdn_new, tail = _gdn_mixer(x2d.reshape(batch, seq, d), lw["attn_norm_w"], lw["w_gdn"], lw["w_ba"],
                                        lw["conv_w"], lw["alog_row"], lw["dtb_row"], lw["gnw_row"],
                                        conv0, s_gdn.astype(F32))
    mix_r, s_ret_new = _ret_mixer(x2d.reshape(batch, seq, d), lw["attn_norm_w"], lw["w_ret"], cos, sin,
                                  lw["ret_gnw_row"], s_ret.astype(F32))
    mix_a = mix_a.reshape(batch * seq, GDN_V)
    mix_r = mix_r.reshape(batch * seq, RET_V)
    x1, h2 = _out_proj(x2d, mix_a, mix_r, lw["w_out_a"], lw["w_out_r"], lw["ffn_norm_w"])
    y = _ffn(h2, x1, lw["w_gate_up"], lw["w_down"], final_w, final_norm)
    rows = tail[:, CARRY_ROWS - (CONV_W - 1):, :].astype(conv_state.dtype)
    return y, rows, s_gdn_new.astype(s_gdn.dtype), s_ret_new.astype(s_ret.dtype)


def kernel(x_prompt, x_sample, state_gdn_conv, state_gdn, state_ret, attn_norm_w, w_in, conv_w, a_log, dt_bias, gdn_norm_w, ret_gn_w, w_out, ffn_norm_w, w_gate_up, w_down, final_norm_w):
    depth = w_in.shape[0]
    bp, lp, d = x_prompt.shape
    bs, ls, _ = x_sample.shape
    assert lp >= CONV_W - 1 and ls >= CONV_W - 1
    inv = 1.0 / (ROPE_BASE ** jnp.linspace(0.0, 1.0, RET_DK // 2, dtype=F32))
    inv_row = inv.reshape(1, LANES)
    cos_p, sin_p = _rope_tables(inv_row, lp, 0)
    cos_s, sin_s = _rope_tables(inv_row, ls, PAST_LEN)
    final_w = final_norm_w.reshape(1, d)
    hp = x_prompt.reshape(bp * lp, d)
    hs = x_sample.reshape(bs * ls, d)
    outs = [[] for _ in range(6)]
    for l in range(depth):
        lw = _prep_layer_weights(w_in[l], w_out[l], w_gate_up[l], w_down[l], a_log[l], dt_bias[l],
                                 gdn_norm_w[l], ret_gn_w[l], attn_norm_w[l], ffn_norm_w[l], conv_w[l])
        last = l == depth - 1
        conv0 = jnp.zeros((bp, CONV_W - 1, GDN_CONV_DIM), x_prompt.dtype)
        sg0 = jnp.zeros((bp, GDN_HEADS, GDN_DK, GDN_DV), state_gdn.dtype)
        sr0 = jnp.zeros((bp, RET_HEADS, RET_DK, RET_DV), state_ret.dtype)
        hp, c_p, g_p, r_p = _layer(hp, bp, lp, cos_p, sin_p, conv0, sg0, sr0, lw, final_w, last)
        hs, c_s, g_s, r_s = _layer(hs, bs, ls, cos_s, sin_s, state_gdn_conv[l], state_gdn[l],
                                   state_ret[l], lw, final_w, last)
        for acc, val in zip(outs, (c_p, g_p, r_p, c_s, g_s, r_s)):
            acc.append(val)
    return (hp.reshape(bp, lp, d), hs.reshape(bs, ls, d)) + tuple(jnp.stack(o) for o in outs)
```

```python
import functools
import math

import jax
import jax.numpy as jnp
import numpy as np
from jax import lax
from jax.experimental import pallas as pl
from jax.experimental.pallas import tpu as pltpu

F32 = jnp.float32
BF16 = jnp.bfloat16

GDN_HEADS = 8
GDN_DK = 128
GDN_DV = 128
CONV_W = 4
RET_HEADS = 4
RET_DK = 256
RET_DV = 256
GDN_QK = GDN_HEADS * GDN_DK
GDN_V = GDN_HEADS * GDN_DV
GDN_CONV_DIM = 2 * GDN_QK + GDN_V
RET_QK = RET_HEADS * RET_DK
RET_V = RET_HEADS * RET_DV
CHUNK = 64
PAST_LEN = 4096
RMS_EPS = 1e-6
GN_EPS = 1e-5
L2_EPS = 1e-6
ROPE_BASE = 10000.0

LANES = 128
SUBLANES = 8
V7X_VMEM_BYTES = 64 * 1024 * 1024
VMEM_LIMIT_BYTES = 52 * 1024 * 1024
VMEM_LIMIT_BIG_TILE_BYTES = 60 * 1024 * 1024
TRI_BLOCK = 16

P_WIDTH = 8 * 1024
BA_WIDTH = 2 * LANES

OUT_PROJ_TM = 512
FFN_TM, FFN_TF = 1024, 512
FFN_TF_FEW_ROWS = 1408


def _bdot(a, b):
    return jnp.dot(a.astype(BF16), b.astype(BF16), preferred_element_type=F32)


def _bdot_nt(a, b):
    return lax.dot_general(a.astype(BF16), b.astype(BF16), (((1,), (1,)), ((), ())),
                           preferred_element_type=F32)


def _bdot_tn(a, b):
    return lax.dot_general(a.astype(BF16), b.astype(BF16), (((0,), (0,)), ((), ())),
                           preferred_element_type=F32)


def _select_rows_exact(sel, x):
    hi = x.astype(BF16)
    rest = x - hi.astype(F32)
    mid = rest.astype(BF16)
    lo = (rest - mid.astype(F32)).astype(BF16)
    n = x.shape[1]
    parts = jnp.dot(sel, jnp.concatenate([hi, mid, lo], axis=1), preferred_element_type=F32)
    return parts[:, 0:n] + parts[:, n:2 * n] + parts[:, 2 * n:3 * n]


def _sigmoid(x):
    return 1.0 / (1.0 + jnp.exp(-x))


def _silu(x):
    return x * _sigmoid(x)


def _rms_scale(x):
    return lax.rsqrt(jnp.mean(x * x, axis=-1, keepdims=True) + RMS_EPS)


def _params(n_grid_dims, vmem_limit_bytes=VMEM_LIMIT_BYTES):
    return pltpu.CompilerParams(dimension_semantics=("arbitrary",) * n_grid_dims,
                                vmem_limit_bytes=vmem_limit_bytes)


M_CAUSAL, M_STRICT, M_EYE, M_DIAG, M_MERGE0 = 0, 1, 2, 3, 4
CARRY_ROWS = 16
GDN_ROWS_PER_STEP = 256


def _gdn_constants(c):
    r = np.arange(c)[:, None]
    q = np.arange(c)[None, :]
    masks = [r >= q, r > q, r == q, (r // TRI_BLOCK) == (q // TRI_BLOCK)]
    blk = TRI_BLOCK
    while blk < c:
        masks.append(((r // (2 * blk)) == (q // (2 * blk))) & ((r // blk) != (q // blk)))
        blk *= 2
    masks = np.stack(masks).astype(np.float32)
    shift = np.zeros(((CONV_W - 1) * c, CARRY_ROWS + c), np.float32)
    for k in range(CONV_W - 1):
        shift[k * c + np.arange(c), CARRY_ROWS + np.arange(c) - (CONV_W - 1) + k] = 1.0
    return jnp.asarray(masks), jnp.asarray(shift, dtype=BF16)


def _tri_inv(a_list, masks_ref, c):
    pw = [a * masks_ref[M_DIAG] for a in a_list]
    p = [masks_ref[M_EYE] - d for d in pw]
    for _ in range(int(math.log2(TRI_BLOCK)) - 1):
        pw = [_bdot(x, x) for x in pw]
        p = [x + _bdot(x, y) for x, y in zip(p, pw)]
    blk, level = TRI_BLOCK, M_MERGE0
    while blk < c:
        t = [_bdot(x, a * masks_ref[level]) for x, a in zip(p, a_list)]
        p = [x - _bdot(y, x) for x, y in zip(p, t)]
        blk, level = 2 * blk, level + 1
    return p


GDN_P_WIDTH = GDN_CONV_DIM + GDN_V
GDN_PROJ_TILE = 512


def _gdn_kernel(x_ref, nw_ref, w_ref, wba_ref, convw_ref, alog_ref, dtb_ref, gnw_ref, conv0_ref, s0_ref,
                masks_ref, shift_ref, o_ref, s_ref, tail_ref, xbuf_ref, p_buf, ba_buf, *, c, bb):
    t = pl.program_id(1)
    nslot = t % 2
    pslot = 1 - nslot

    @pl.when(t == 0)
    def _():
        p_buf[1] = jnp.zeros(p_buf.shape[1:], BF16)
        ba_buf[1] = jnp.zeros(ba_buf.shape[1:], F32)
        xbuf_ref[...] = jnp.zeros(xbuf_ref.shape, BF16)
        s_ref[...] = s0_ref[...]

    @pl.when(t == 1)
    def _():
        xbuf_ref[:, 0:CARRY_ROWS, :] = conv0_ref[...]
        s_ref[...] = s0_ref[...]

    heads = range(GDN_HEADS)
    seqs = range(bb)
    col = lambda arr, h: arr[:, h:h + 1]
    lanes = lambda base, h: slice(base + h * GDN_DK, base + (h + 1) * GDN_DK)
    rows = lambda arr, h: arr[h * c:(h + 1) * c, :]
    seq_rows = lambda i: slice(i * c, (i + 1) * c)

    x = x_ref[...].reshape(bb * c, x_ref.shape[-1])
    h_next = (x * _rms_scale(x) * nw_ref[...]).astype(BF16)
    proj_tiles = [slice(lo, lo + GDN_PROJ_TILE) for lo in range(0, GDN_P_WIDTH, GDN_PROJ_TILE)]

    def project(tiles):
        for cols in tiles:
            p_buf[nslot, :, cols] = jnp.dot(h_next, w_ref[:, cols], preferred_element_type=F32).astype(BF16)

    ba_buf[nslot] = jnp.dot(h_next, wba_ref[...], preferred_element_type=F32)

    qkv = lambda i: p_buf[pslot, seq_rows(i), 0:GDN_CONV_DIM]

    def shifted_rows(i):
        xbuf_ref[i, CARRY_ROWS:CARRY_ROWS + c, :] = qkv(i)
        shifted = jnp.dot(shift_ref[...], xbuf_ref[i], preferred_element_type=F32)
        xbuf_ref[i, 0:CARRY_ROWS, :] = xbuf_ref[i, c:c + CARRY_ROWS, :]
        return shifted

    def gates(i):
        ba = ba_buf[pslot, seq_rows(i), :]
        beta = _sigmoid(ba[:, 0:LANES])
        sp_in = ba[:, LANES:2 * LANES] + dtb_ref[...]
        softplus = jnp.maximum(sp_in, 0.0) + jnp.log(1.0 + jnp.exp(-jnp.abs(sp_in)))
        gc = _select_rows_exact(masks_ref[M_CAUSAL].astype(BF16), -jnp.exp(alog_ref[...]) * softplus)
        gc_t = jnp.concatenate([gc, jnp.zeros((LANES - c, LANES), F32)], axis=0).T
        g_last = gc[c - 1:c, :]
        e_gc = jnp.exp(gc)
        return dict(beta=beta, gc=gc, gc_t=gc_t, e_gc=e_gc, e_kd=jnp.exp(g_last - gc),
                    e_last=jnp.exp(g_last), beta_e=beta * e_gc)

    def conv_taps(i, shifted):
        w = convw_ref[...]
        y = shifted[0:c, :] * w[0:1, :]
        y = y + shifted[c:2 * c, :] * w[1:2, :]
        y = y + shifted[2 * c:3 * c, :] * w[2:3, :]
        y = y + qkv(i).astype(F32) * w[3:4, :]
        return _silu(y)

    def inv_norm(y, base, fill):
        sq = jnp.concatenate([jnp.square(y[:, lanes(base, h)]) for h in heads], axis=0)
        tot = _bdot(sq, jnp.full((GDN_DK, LANES), fill, F32))
        return lax.rsqrt(tot + fill * L2_EPS)

    n_tiles = len(proj_tiles)
    shifted = [shifted_rows(i) for i in seqs]
    project(proj_tiles[0:n_tiles // 2])
    pre = [gates(i) for i in seqs]
    y = [conv_taps(i, shifted[i]) for i in seqs]
    project(proj_tiles[n_tiles // 2:])
    inv_q = [inv_norm(y[i], 0, float(GDN_DK)) for i in seqs]
    inv_k = [inv_norm(y[i], GDN_QK, 1.0) for i in seqs]
    probs = [(i, h) for i in seqs for h in heads]
    qn = [y[i][:, lanes(0, h)] * rows(inv_q[i], h) for i, h in probs]
    kn = [y[i][:, lanes(GDN_QK, h)] * rows(inv_k[i], h) for i, h in probs]
    kb = [kn[j] * col(pre[i]["beta"], h) for j, (i, h) in enumerate(probs)]
    qk = [_bdot_nt(jnp.concatenate([qn[j], kb[j]], axis=0), kn[j]) for j in range(len(probs))]
    attn, a_mat = [], []
    for j, (i, h) in enumerate(probs):
        diff = col(pre[i]["gc"], h) - pre[i]["gc_t"][h:h + 1, 0:c]
        decay = jnp.exp(jnp.minimum(diff, 0.0))
        attn.append(qk[j][0:c, :] * (decay * masks_ref[M_CAUSAL]))
        a_mat.append(qk[j][c:2 * c, :] * (decay * masks_ref[M_STRICT]))
    t_inv = _tri_inv(a_mat, masks_ref, c)
    w_u = [_bdot(t_inv[j], jnp.concatenate([y[i][:, lanes(2 * GDN_QK, h)] * col(pre[i]["beta"], h),
                                            kn[j] * col(pre[i]["beta_e"], h)], axis=1))
           for j, (i, h) in enumerate(probs)]
    x_s = [_bdot(jnp.concatenate([w_u[j][:, GDN_DV:2 * GDN_DV], qn[j] * col(pre[i]["e_gc"], h)], axis=0),
                 s_ref[i, h]) for j, (i, h) in enumerate(probs)]
    u = [w_u[j][:, 0:GDN_DV] - x_s[j][0:c, :] for j in range(len(probs))]
    o = [x_s[j][c:2 * c, :] + _bdot(attn[j], u[j]) for j in range(len(probs))]
    ds = [_bdot_tn(kn[j] * col(pre[i]["e_kd"], h), u[j]) for j, (i, h) in enumerate(probs)]
    gnw = gnw_ref[...]
    for j, (i, h) in enumerate(probs):
        s_ref[i, h] = s_ref[i, h] * col(pre[i]["e_last"], h) + ds[j]
        out = o[j] * _rms_scale(o[j]) * gnw
        z = p_buf[pslot, seq_rows(i), GDN_CONV_DIM + h * GDN_DV:GDN_CONV_DIM + (h + 1) * GDN_DV]
        out = out * _silu(z.astype(F32))
        o_ref[i, :, lanes(0, h)] = out.astype(BF16)
    tail_ref[...] = xbuf_ref[:, 0:CARRY_ROWS, :]


def _gdn_mixer(x3, norm_w, w_gdn, w_ba, conv_w, alog_row, dtb_row, gnw_row, conv0, s0):
    batch, seq, d = x3.shape
    c = min(CHUNK, seq)
    bb = min(GDN_ROWS_PER_STEP // c, batch)
    n = seq // c
    masks, shift = _gdn_constants(c)
    const = lambda shape: pl.BlockSpec(shape, lambda b, t: (0,) * len(shape))
    state_spec = pl.BlockSpec((bb, GDN_HEADS, GDN_DK, GDN_DV), lambda b, t: (b, 0, 0, 0))
    carry_spec = pl.BlockSpec((bb, CARRY_ROWS, GDN_CONV_DIM), lambda b, t: (b, 0, 0))
    return pl.pallas_call(
        functools.partial(_gdn_kernel, c=c, bb=bb),
        grid=(batch // bb, n + 1),
        in_specs=[
            pl.BlockSpec((bb, c, d), lambda b, t: (b, jnp.minimum(t, n - 1), 0)),
            const((1, d)),
            const(w_gdn.shape),
            const(w_ba.shape),
            const((CONV_W, GDN_CONV_DIM)),
            const((1, LANES)),
            const((1, LANES)),
            const((1, GDN_DV)),
            carry_spec,
            state_spec,
            const(masks.shape),
            const(shift.shape),
        ],
        out_specs=[
            pl.BlockSpec((bb, c, GDN_V), lambda b, t: (b, jnp.maximum(t - 1, 0), 0)),
            state_spec,
            carry_spec,
        ],
        out_shape=[
            jax.ShapeDtypeStruct((batch, seq, GDN_V), BF16),
            jax.ShapeDtypeStruct((batch, GDN_HEADS, GDN_DK, GDN_DV), F32),
            jax.ShapeDtypeStruct((batch, CARRY_ROWS, GDN_CONV_DIM), BF16),
        ],
        scratch_shapes=[
            pltpu.VMEM((bb, CARRY_ROWS + c, GDN_CONV_DIM), BF16),
            pltpu.VMEM((2, bb * c, GDN_P_WIDTH), BF16),
            pltpu.VMEM((2, bb * c, BA_WIDTH), F32),
        ],
        compiler_params=_params(2, VMEM_LIMIT_BIG_TILE_BYTES),
        name="gdn_mixer",
    )(x3, norm_w, w_gdn, w_ba, conv_w, alog_row, dtb_row, gnw_row, conv0, s0, masks, shift)


def _rope_kernel(inv_ref, cos_ref, sin_ref, *, rows, offset):
    pos = lax.broadcasted_iota(jnp.int32, (rows, LANES), 0) + (pl.program_id(0) * rows + offset)
    ang = pos.astype(F32) * inv_ref[...]
    cos_ref[...] = jnp.cos(ang)
    sin_ref[...] = jnp.sin(ang)


def _rope_tables(inv_row, seq, offset):
    rows = min(512, seq)
    return pl.pallas_call(
        functools.partial(_rope_kernel, rows=rows, offset=offset),
        grid=(seq // rows,),
        in_specs=[pl.BlockSpec((1, LANES), lambda i: (0, 0))],
        out_specs=[pl.BlockSpec((rows, LANES), lambda i: (i, 0))] * 2,
        out_shape=[jax.ShapeDtypeStruct((seq, LANES), F32)] * 2,
        compiler_params=_params(1),
        name="rope_tables",
    )(inv_row)


RET_CHUNK = 256
RET_ROWS_PER_STEP = 512


def _pair_split_matrix():
    perm = np.concatenate([np.arange(0, RET_DK, 2), np.arange(1, RET_DK, 2)])
    return np.eye(RET_DK, dtype=np.float32)[:, perm]


RET_P_WIDTH = 2 * RET_QK + 2 * RET_V
RET_PROJ_TILE = 512


def _ret_kernel(x_ref, nw_ref, w_ref, cos_ref, sin_ref, gnw_ref, s0_ref, split_ref, split_t_ref,
                o_ref, s_ref, dmat_ref, scale_ref, p_buf, *, c, bb, n_blocks):
    t = pl.program_id(1)
    nslot = t % 2
    pslot = 1 - nslot
    probs = [(i, h) for i in range(bb) for h in range(RET_HEADS)]

    def load_state():
        for i, h in probs:
            s_ref[i, h] = _select_rows_exact(split_t_ref[...], s0_ref[i, h])

    @pl.when(t == 0)
    def _():
        p_buf[1] = jnp.zeros(p_buf.shape[1:], BF16)
        load_state()

    pl.when(t == 1)(load_state)

    @pl.when(jnp.logical_and(pl.program_id(0) == 0, t == 0))
    def _():
        r = lax.broadcasted_iota(jnp.int32, (c, c), 0)
        q_i = lax.broadcasted_iota(jnp.int32, (c, c), 1)
        causal = r >= q_i
        dpos = (r - q_i).astype(F32)
        tok = lax.broadcasted_iota(jnp.int32, (c, LANES), 0).astype(F32)
        for h in range(RET_HEADS):
            lg = math.log(1.0 - 2.0 ** (-5.0 - h))
            dmat = jnp.where(causal, jnp.exp(jnp.where(causal, dpos * lg, 0.0)), 0.0)
            dmat_ref[h] = dmat * (RET_DK ** -0.5)
            scale_ref[0, h] = jnp.exp((tok + 1.0) * lg)
            scale_ref[1, h] = jnp.exp((c - 1.0 - tok) * lg) * (RET_DK ** -0.5)

    x = x_ref[...].reshape(bb * c, x_ref.shape[-1])
    h_next = (x * _rms_scale(x) * nw_ref[...]).astype(BF16)
    proj_tiles = [slice(lo, lo + RET_PROJ_TILE) for lo in range(0, RET_P_WIDTH, RET_PROJ_TILE)]

    def project(tiles):
        for cols in tiles:
            p_buf[nslot, :, cols] = jnp.dot(h_next, w_ref[:, cols], preferred_element_type=F32).astype(BF16)

    cos = cos_ref[...]
    sin = sin_ref[...]
    half = RET_DK // 2

    def rotate(x):
        xe, xo = x[:, 0:half], x[:, half:RET_DK]
        return jnp.concatenate([xe * cos - xo * sin, xo * cos + xe * sin], axis=1)

    both = lambda arr: jnp.concatenate([arr, arr], axis=1)
    prev = lambda g, i, h: p_buf[pslot, i * c:(i + 1) * c, g * RET_QK + h * RET_DK:g * RET_QK + (h + 1) * RET_DK]
    lanes = lambda h: slice(h * RET_DK, (h + 1) * RET_DK)
    n_tiles = len(proj_tiles)
    project(proj_tiles[0:n_tiles // 2])
    qr = [rotate(prev(0, i, h).astype(F32)) for i, h in probs]
    kr = [rotate(prev(1, i, h).astype(F32)) for i, h in probs]
    attn = [_bdot_nt(qr[j], kr[j]) * dmat_ref[h] for j, (i, h) in enumerate(probs)]
    o = [_bdot(qr[j] * both(scale_ref[0, h]), s_ref[i, h]) + _bdot(attn[j], prev(2, i, h))
         for j, (i, h) in enumerate(probs)]
    ds = [_bdot_tn(kr[j] * both(scale_ref[1, h]), prev(2, i, h)) for j, (i, h) in enumerate(probs)]
    project(proj_tiles[n_tiles // 2:])
    for j, (i, h) in enumerate(probs):
        s_ref[i, h] = s_ref[i, h] * math.exp(c * math.log(1.0 - 2.0 ** (-5.0 - h))) + ds[j]
        mu = jnp.mean(o[j], axis=-1, keepdims=True)
        dev = o[j] - mu
        var = jnp.mean(jnp.square(dev), axis=-1, keepdims=True)
        out = dev * lax.rsqrt(var + GN_EPS) * gnw_ref[:, lanes(h)]
        out = out * _silu(prev(3, i, h).astype(F32))
        o_ref[i, :, lanes(h)] = out.astype(BF16)

    @pl.when(t == n_blocks)
    def _():
        for i, h in probs:
            s_ref[i, h] = _select_rows_exact(split_ref[...], s_ref[i, h])


def _ret_mixer(x3, norm_w, w_ret, cos, sin, gnw_row, s0):
    batch, seq, d = x3.shape
    c = min(RET_CHUNK, seq)
    bb = min(RET_ROWS_PER_STEP // c, batch)
    n = seq // c
    split = _pair_split_matrix()
    const = lambda shape: pl.BlockSpec(shape, lambda b, t: (0,) * len(shape))
    state_spec = pl.BlockSpec((bb, RET_HEADS, RET_DK, RET_DV), lambda b, t: (b, 0, 0, 0))
    prev_rows = lambda b, t: (jnp.maximum(t - 1, 0), 0)
    return pl.pallas_call(
        functools.partial(_ret_kernel, c=c, bb=bb, n_blocks=n),
        grid=(batch // bb, n + 1),
        in_specs=[
            pl.BlockSpec((bb, c, d), lambda b, t: (b, jnp.minimum(t, n - 1), 0)),
            const((1, d)),
            const(w_ret.shape),
            pl.BlockSpec((c, LANES), prev_rows),
            pl.BlockSpec((c, LANES), prev_rows),
            const((1, RET_V)),
            state_spec,
            const((RET_DK, RET_DK)),
            const((RET_DK, RET_DK)),
        ],
        out_specs=[
            pl.BlockSpec((bb, c, RET_V), lambda b, t: (b, jnp.maximum(t - 1, 0), 0)),
            state_spec,
        ],
        out_shape=[
            jax.ShapeDtypeStruct((batch, seq, RET_V), BF16),
            jax.ShapeDtypeStruct((batch, RET_HEADS, RET_DK, RET_DV), F32),
        ],
        scratch_shapes=[pltpu.VMEM((RET_HEADS, c, c), F32), pltpu.VMEM((2, RET_HEADS, c, LANES), F32),
                        pltpu.VMEM((2, bb * c, RET_P_WIDTH), BF16)],
        compiler_params=_params(2, VMEM_LIMIT_BIG_TILE_BYTES),
        name="ret_mixer",
    )(x3, norm_w, w_ret, cos, sin, gnw_row, s0, jnp.asarray(split, dtype=BF16), jnp.asarray(split.T, dtype=BF16))


def _out_proj_kernel(x_ref, ma_ref, mr_ref, woa_ref, wor_ref, nw_ref, x1_ref, h2_ref):
    x1 = x_ref[...] + jnp.dot(ma_ref[...], woa_ref[...], preferred_element_type=F32)
    x1 = x1 + jnp.dot(mr_ref[...], wor_ref[...], preferred_element_type=F32)
    x1_ref[...] = x1
    h2_ref[...] = (x1 * _rms_scale(x1) * nw_ref[...]).astype(BF16)


def _out_proj(x2d, mix_a, mix_r, w_out_a, w_out_r, norm_w):
    t, d = x2d.shape
    tm = min(OUT_PROJ_TM, t)
    return pl.pallas_call(
        _out_proj_kernel,
        grid=(t // tm,),
        in_specs=[
            pl.BlockSpec((tm, d), lambda i: (i, 0)),
            pl.BlockSpec((tm, GDN_V), lambda i: (i, 0)),
            pl.BlockSpec((tm, RET_V), lambda i: (i, 0)),
            pl.BlockSpec((GDN_V, d), lambda i: (0, 0)),
            pl.BlockSpec((RET_V, d), lambda i: (0, 0)),
            pl.BlockSpec((1, d), lambda i: (0, 0)),
        ],
        out_specs=[
            pl.BlockSpec((tm, d), lambda i: (i, 0)),
            pl.BlockSpec((tm, d), lambda i: (i, 0)),
        ],
        out_shape=[
            jax.ShapeDtypeStruct((t, d), F32),
            jax.ShapeDtypeStruct((t, d), BF16),
        ],
        compiler_params=_params(1),
        name="out_proj",
    )(x2d, mix_a, mix_r, w_out_a, w_out_r, norm_w)


def _ffn_kernel(h2_ref, x1_ref, wg_ref, wu_ref, wd_ref, fw_ref, y_ref, *, n_ff, final_norm):
    f = pl.program_id(1)

    @pl.when(f == 0)
    def _():
        y_ref[...] = x1_ref[...]

    h = h2_ref[...]
    gate = jnp.dot(h, wg_ref[...], preferred_element_type=F32)
    up = jnp.dot(h, wu_ref[...], preferred_element_type=F32)
    act = (_silu(gate) * up).astype(BF16)
    y_ref[...] += jnp.dot(act, wd_ref[...], preferred_element_type=F32)

    if final_norm:
        @pl.when(f == n_ff - 1)
        def _():
            x2 = y_ref[...]
            y_ref[...] = x2 * _rms_scale(x2) * fw_ref[...]


def _ffn(h2, x1, w_gate_up, w_down, final_w, final_norm):
    t, d = x1.shape
    d_ff = w_down.shape[0]
    tm = min(FFN_TM, t)
    tf = FFN_TF if tm == FFN_TM else FFN_TF_FEW_ROWS
    n_ff = d_ff // tf
    return pl.pallas_call(
        functools.partial(_ffn_kernel, n_ff=n_ff, final_norm=final_norm),
        grid=(t // tm, n_ff),
        in_specs=[
            pl.BlockSpec((tm, d), lambda i, f: (i, 0)),
            pl.BlockSpec((tm, d), lambda i, f: (i, 0)),
            pl.BlockSpec((d, tf), lambda i, f: (0, f)),
            pl.BlockSpec((d, tf), lambda i, f: (0, n_ff + f)),
            pl.BlockSpec((tf, d), lambda i, f: (f, 0)),
            pl.BlockSpec((1, d), lambda i, f: (0, 0)),
        ],
        out_specs=pl.BlockSpec((tm, d), lambda i, f: (i, 0)),
        out_shape=jax.ShapeDtypeStruct((t, d), F32),
        compiler_params=_params(2, VMEM_LIMIT_BIG_TILE_BYTES),
        name="ffn",
    )(h2, x1, w_gate_up, w_gate_up, w_down, final_w)


W_IN_PREP_ROWS = 256


def _w_in_layout_kernel(w_ref, split_ref, gdn_ref, ret_ref, ba_ref):
    o_b = GDN_CONV_DIM + GDN_V
    o_qr = o_b + 2 * GDN_HEADS
    gdn_ref[...] = w_ref[:, 0:o_b]
    for head in range(2 * RET_HEADS):
        src = w_ref[:, o_qr + head * RET_DK:o_qr + (head + 1) * RET_DK]
        ret_ref[:, head * RET_DK:(head + 1) * RET_DK] = jnp.dot(
            src, split_ref[...], preferred_element_type=F32).astype(BF16)
    ret_ref[:, 2 * RET_QK:RET_P_WIDTH] = w_ref[:, o_qr + 2 * RET_QK:o_qr + RET_P_WIDTH]
    gates = w_ref[:, o_b:o_qr]
    pad = jnp.zeros((gates.shape[0], LANES - GDN_HEADS), BF16)
    ba_ref[...] = jnp.concatenate([gates[:, 0:GDN_HEADS], pad, gates[:, GDN_HEADS:2 * GDN_HEADS], pad], axis=1)


def _w_in_layout(w_in_bf16):
    d, width = w_in_bf16.shape
    rows = min(W_IN_PREP_ROWS, d)
    out = lambda n: pl.BlockSpec((rows, n), lambda i: (i, 0))
    return pl.pallas_call(
        _w_in_layout_kernel,
        grid=(d // rows,),
        in_specs=[pl.BlockSpec((rows, width), lambda i: (i, 0)),
                  pl.BlockSpec((RET_DK, RET_DK), lambda i: (0, 0))],
        out_specs=[out(GDN_P_WIDTH), out(RET_P_WIDTH), out(BA_WIDTH)],
        out_shape=[jax.ShapeDtypeStruct((d, n), BF16) for n in (GDN_P_WIDTH, RET_P_WIDTH, BA_WIDTH)],
        compiler_params=_params(1),
        name="w_in_layout",
    )(w_in_bf16, jnp.asarray(_pair_split_matrix(), dtype=BF16))


def _prep_layer_weights(w_in, w_out, w_gate_up, w_down, a_log, dt_bias, gdn_norm_w, ret_gn_w,
                        attn_norm_w, ffn_norm_w, conv_w):
    d = w_in.shape[0]
    w_gdn, w_ret, w_ba = _w_in_layout(w_in.astype(BF16))
    row_pad = lambda v: jnp.pad(v.astype(F32), (0, LANES - v.shape[0])).reshape(1, LANES)
    return dict(
        w_gdn=w_gdn, w_ret=w_ret, w_ba=w_ba,
        w_out_a=w_out[0:GDN_V].astype(BF16), w_out_r=w_out[GDN_V:].astype(BF16),
        w_gate_up=w_gate_up.astype(BF16), w_down=w_down.astype(BF16),
        alog_row=row_pad(a_log), dtb_row=row_pad(dt_bias),
        gnw_row=gdn_norm_w.reshape(1, GDN_DV).astype(F32),
        ret_gnw_row=ret_gn_w.reshape(1, RET_V).astype(F32),
        attn_norm_w=attn_norm_w.reshape(1, d), ffn_norm_w=ffn_norm_w.reshape(1, d),
        conv_w=conv_w,
    )


def _layer(x2d, batch, seq, cos, sin, conv_state, s_gdn, s_ret, lw, final_w, final_norm):
    d = x2d.shape[1]
    conv0 = jnp.pad(conv_state.astype(BF16), ((0, 0), (CARRY_ROWS - (CONV_W - 1), 0), (0, 0)))
    mix_a, s_gdn_new, tail = _gdn_mixer(x2d.reshape(batch, seq, d), lw["attn_norm_w"], lw["w_gdn"], lw["w_ba"],
                                        lw["conv_w"], lw["alog_row"], lw["dtb_row"], lw["gnw_row"],
                                        conv0, s_gdn.astype(F32))
    mix_r, s_ret_new = _ret_mixer(x2d.reshape(batch, seq, d), lw["attn_norm_w"], lw["w_ret"], cos, sin,
                                  lw["ret_gnw_row"], s_ret.astype(F32))
    mix_a = mix_a.reshape(batch * seq, GDN_V)
    mix_r = mix_r.reshape(batch * seq, RET_V)
    x1, h2 = _out_proj(x2d, mix_a, mix_r, lw["w_out_a"], lw["w_out_r"], lw["ffn_norm_w"])
    y = _ffn(h2, x1, lw["w_gate_up"], lw["w_down"], final_w, final_norm)
    rows = tail[:, CARRY_ROWS - (CONV_W - 1):, :].astype(conv_state.dtype)
    return y, rows, s_gdn_new.astype(s_gdn.dtype), s_ret_new.astype(s_ret.dtype)


def kernel(x_prompt, x_sample, state_gdn_conv, state_gdn, state_ret, attn_norm_w, w_in, conv_w, a_log, dt_bias, gdn_norm_w, ret_gn_w, w_out, ffn_norm_w, w_gate_up, w_down, final_norm_w):
    depth = w_in.shape[0]
    bp, lp, d = x_prompt.shape
    bs, ls, _ = x_sample.shape
    assert lp >= CONV_W - 1 and ls >= CONV_W - 1
    inv = 1.0 / (ROPE_BASE ** jnp.linspace(0.0, 1.0, RET_DK // 2, dtype=F32))
    inv_row = inv.reshape(1, LANES)
    cos_p, sin_p = _rope_tables(inv_row, lp, 0)
    cos_s, sin_s = _rope_tables(inv_row, ls, PAST_LEN)
    final_w = final_norm_w.reshape(1, d)
    hp = x_prompt.reshape(bp * lp, d)
    hs = x_sample.reshape(bs * ls, d)
    outs = [[] for _ in range(6)]
    for l in range(depth):
        lw = _prep_layer_weights(w_in[l], w_out[l], w_gate_up[l], w_down[l], a_log[l], dt_bias[l],
                                 gdn_norm_w[l], ret_gn_w[l], attn_norm_w[l], ffn_norm_w[l], conv_w[l])
        last = l == depth - 1
        conv0 = jnp.zeros((bp, CONV_W - 1, GDN_CONV_DIM), x_prompt.dtype)
        sg0 = jnp.zeros((bp, GDN_HEADS, GDN_DK, GDN_DV), state_gdn.dtype)
        sr0 = jnp.zeros((bp, RET_HEADS, RET_DK, RET_DV), state_ret.dtype)
        hp, c_p, g_p, r_p = _layer(hp, bp, lp, cos_p, sin_p, conv0, sg0, sr0, lw, final_w, last)
        hs, c_s, g_s, r_s = _layer(hs, bs, ls, cos_s, sin_s, state_gdn_conv[l], state_gdn[l],
                                   state_ret[l], lw, final_w, last)
        for acc, val in zip(outs, (c_p, g_p, r_p, c_s, g_s, r_s)):
            acc.append(val)
    return (hp.reshape(bp, lp, d), hs.reshape(bs, ls, d)) + tuple(jnp.stack(o) for o in outs)
```

```python
import functools
import math

import jax
import jax.numpy as jnp
import numpy as np
from jax import lax
from jax.experimental import pallas as pl
from jax.experimental.pallas import tpu as pltpu

F32 = jnp.float32
BF16 = jnp.bfloat16

GDN_HEADS = 8
GDN_DK = 128
GDN_DV = 128
CONV_W = 4
RET_HEADS = 4
RET_DK = 256
RET_DV = 256
GDN_QK = GDN_HEADS * GDN_DK
GDN_V = GDN_HEADS * GDN_DV
GDN_CONV_DIM = 2 * GDN_QK + GDN_V
RET_QK = RET_HEADS * RET_DK
RET_V = RET_HEADS * RET_DV
CHUNK = 64
PAST_LEN = 4096
RMS_EPS = 1e-6
GN_EPS = 1e-5
L2_EPS = 1e-6
ROPE_BASE = 10000.0

LANES = 128
SUBLANES = 8
V7X_VMEM_BYTES = 64 * 1024 * 1024
VMEM_LIMIT_BYTES = 52 * 1024 * 1024
VMEM_LIMIT_BIG_TILE_BYTES = 60 * 1024 * 1024
TRI_BLOCK = 16
TRI_PACK_LANES = 256

P_WIDTH = 8 * 1024
BA_WIDTH = 2 * LANES

OUT_PROJ_TM = 512
FFN_TM, FFN_TF = 1024, 512
FFN_TF_FEW_ROWS = 1408


def _bdot(a, b):
    return jnp.dot(a.astype(BF16), b.astype(BF16), preferred_element_type=F32)


def _bdot_nt(a, b):
    return lax.dot_general(a.astype(BF16), b.astype(BF16), (((1,), (1,)), ((), ())),
                           preferred_element_type=F32)


def _bdot_tn(a, b):
    return lax.dot_general(a.astype(BF16), b.astype(BF16), (((0,), (0,)), ((), ())),
                           preferred_element_type=F32)


def _select_rows_exact(sel, x):
    hi = x.astype(BF16)
    rest = x - hi.astype(F32)
    mid = rest.astype(BF16)
    lo = (rest - mid.astype(F32)).astype(BF16)
    n = x.shape[1]
    parts = jnp.dot(sel, jnp.concatenate([hi, mid, lo], axis=1), preferred_element_type=F32)
    return parts[:, 0:n] + parts[:, n:2 * n] + parts[:, 2 * n:3 * n]


def _sigmoid(x):
    return 1.0 / (1.0 + jnp.exp(-x))


def _silu(x):
    return x * _sigmoid(x)


def _rms_scale(x):
    return lax.rsqrt(jnp.mean(x * x, axis=-1, keepdims=True) + RMS_EPS)


def _params(n_grid_dims, vmem_limit_bytes=VMEM_LIMIT_BYTES):
    return pltpu.CompilerParams(dimension_semantics=("arbitrary",) * n_grid_dims,
                                vmem_limit_bytes=vmem_limit_bytes)


M_CAUSAL, M_STRICT, M_EYE, M_DIAG, M_MERGE0 = 0, 1, 2, 3, 4
CARRY_ROWS = 16
GDN_ROWS_PER_STEP = 256


def _gdn_constants(c):
    r = np.arange(c)[:, None]
    q = np.arange(c)[None, :]
    masks = [r >= q, r > q, r == q, (r // TRI_BLOCK) == (q // TRI_BLOCK)]
    blk = TRI_BLOCK
    while blk < c:
        masks.append(((r // (2 * blk)) == (q // (2 * blk))) & ((r // blk) != (q // blk)))
        blk *= 2
    masks = np.stack(masks).astype(np.float32)
    shift = np.zeros(((CONV_W - 1) * c, CARRY_ROWS + c), np.float32)
    for k in range(CONV_W - 1):
        shift[k * c + np.arange(c), CARRY_ROWS + np.arange(c) - (CONV_W - 1) + k] = 1.0
    return jnp.asarray(masks), jnp.asarray(shift, dtype=BF16)


def _tri_inv(a_list, c):
    group = TRI_PACK_LANES // c
    width = group * c
    r = lax.broadcasted_iota(jnp.int32, (c, width), 0)
    q = lax.broadcasted_iota(jnp.int32, (c, width), 1) % c
    eye = (r == q).astype(F32)
    diag = ((r // TRI_BLOCK) == (q // TRI_BLOCK)).astype(F32)
    same_block = ((lax.broadcasted_iota(jnp.int32, (width, width), 0) // c)
                  == (lax.broadcasted_iota(jnp.int32, (width, width), 1) // c)).astype(F32).astype(BF16)

    def rmul(x, y):
        y_diag = jnp.concatenate([y.astype(BF16)] * group, axis=0) * same_block
        return jnp.dot(x.astype(BF16), y_diag, preferred_element_type=F32)

    a_wide = [jnp.concatenate(a_list[g:g + group], axis=1) for g in range(0, len(a_list), group)]
    pw = [a * diag for a in a_wide]
    p = [eye - d for d in pw]
    for _ in range(int(math.log2(TRI_BLOCK)) - 1):
        pw = [rmul(x, x) for x in pw]
        p = [x + rmul(x, y) for x, y in zip(p, pw)]
    blk = TRI_BLOCK
    while blk < c:
        merge = (((r // (2 * blk)) == (q // (2 * blk))) & ((r // blk) != (q // blk))).astype(F32)
        t = [rmul(x, a * merge) for x, a in zip(p, a_wide)]
        p = [x - rmul(y, x) for x, y in zip(p, t)]
        blk *= 2
    return [x[:, j * c:(j + 1) * c] for x in p for j in range(group)]


GDN_P_WIDTH = GDN_CONV_DIM + GDN_V
GDN_PROJ_TILE = 512


def _gdn_kernel(x_ref, nw_ref, w_ref, wba_ref, convw_ref, alog_ref, dtb_ref, gnw_ref, conv0_ref, s0_ref,
                masks_ref, shift_ref, o_ref, s_ref, tail_ref, xbuf_ref, p_buf, ba_buf, *, c, bb):
    t = pl.program_id(1)
    nslot = t % 2
    pslot = 1 - nslot

    @pl.when(t == 0)
    def _():
        p_buf[1] = jnp.zeros(p_buf.shape[1:], BF16)
        ba_buf[1] = jnp.zeros(ba_buf.shape[1:], F32)
        xbuf_ref[...] = jnp.zeros(xbuf_ref.shape, BF16)
        s_ref[...] = s0_ref[...]

    @pl.when(t == 1)
    def _():
        xbuf_ref[:, 0:CARRY_ROWS, :] = conv0_ref[...]
        s_ref[...] = s0_ref[...]

    heads = range(GDN_HEADS)
    seqs = range(bb)
    col = lambda arr, h: arr[:, h:h + 1]
    lanes = lambda base, h: slice(base + h * GDN_DK, base + (h + 1) * GDN_DK)
    rows = lambda arr, h: arr[h * c:(h + 1) * c, :]
    seq_rows = lambda i: slice(i * c, (i + 1) * c)

    x = x_ref[...].reshape(bb * c, x_ref.shape[-1])
    h_next = (x * _rms_scale(x) * nw_ref[...]).astype(BF16)
    proj_tiles = [slice(lo, lo + GDN_PROJ_TILE) for lo in range(0, GDN_P_WIDTH, GDN_PROJ_TILE)]

    def project(tiles):
        for cols in tiles:
            p_buf[nslot, :, cols] = jnp.dot(h_next, w_ref[:, cols], preferred_element_type=F32).astype(BF16)

    ba_buf[nslot] = jnp.dot(h_next, wba_ref[...], preferred_element_type=F32)

    qkv = lambda i: p_buf[pslot, seq_rows(i), 0:GDN_CONV_DIM]

    def shifted_rows(i):
        xbuf_ref[i, CARRY_ROWS:CARRY_ROWS + c, :] = qkv(i)
        shifted = jnp.dot(shift_ref[...], xbuf_ref[i], preferred_element_type=F32)
        xbuf_ref[i, 0:CARRY_ROWS, :] = xbuf_ref[i, c:c + CARRY_ROWS, :]
        return shifted

    def gates(i):
        ba = ba_buf[pslot, seq_rows(i), :]
        beta = _sigmoid(ba[:, 0:LANES])
        sp_in = ba[:, LANES:2 * LANES] + dtb_ref[...]
        softplus = jnp.maximum(sp_in, 0.0) + jnp.log(1.0 + jnp.exp(-jnp.abs(sp_in)))
        gc = _select_rows_exact(masks_ref[M_CAUSAL].astype(BF16), -jnp.exp(alog_ref[...]) * softplus)
        gc_t = jnp.concatenate([gc, jnp.zeros((LANES - c, LANES), F32)], axis=0).T
        g_last = gc[c - 1:c, :]
        e_gc = jnp.exp(gc)
        return dict(beta=beta, gc=gc, gc_t=gc_t, e_gc=e_gc, e_kd=jnp.exp(g_last - gc),
                    e_last=jnp.exp(g_last), beta_e=beta * e_gc)

    def conv_taps(i, shifted):
        w = convw_ref[...]
        y = shifted[0:c, :] * w[0:1, :]
        y = y + shifted[c:2 * c, :] * w[1:2, :]
        y = y + shifted[2 * c:3 * c, :] * w[2:3, :]
        y = y + qkv(i).astype(F32) * w[3:4, :]
        return _silu(y)

    def inv_norm(y, base, fill):
        sq = jnp.concatenate([jnp.square(y[:, lanes(base, h)]) for h in heads], axis=0)
        tot = _bdot(sq, jnp.full((GDN_DK, LANES), fill, F32))
        return lax.rsqrt(tot + fill * L2_EPS)

    n_tiles = len(proj_tiles)
    shifted = [shifted_rows(i) for i in seqs]
    project(proj_tiles[0:n_tiles // 2])
    pre = [gates(i) for i in seqs]
    y = [conv_taps(i, shifted[i]) for i in seqs]
    project(proj_tiles[n_tiles // 2:])
    inv_q = [inv_norm(y[i], 0, float(GDN_DK)) for i in seqs]
    inv_k = [inv_norm(y[i], GDN_QK, 1.0) for i in seqs]
    probs = [(i, h) for i in seqs for h in heads]
    qn = [y[i][:, lanes(0, h)] * rows(inv_q[i], h) for i, h in probs]
    kn = [y[i][:, lanes(GDN_QK, h)] * rows(inv_k[i], h) for i, h in probs]
    kb = [kn[j] * col(pre[i]["beta"], h) for j, (i, h) in enumerate(probs)]
    qk = [_bdot_nt(jnp.concatenate([qn[j], kb[j]], axis=0), kn[j]) for j in range(len(probs))]
    attn, a_mat = [], []
    for j, (i, h) in enumerate(probs):
        diff = col(pre[i]["gc"], h) - pre[i]["gc_t"][h:h + 1, 0:c]
        decay = jnp.exp(jnp.minimum(diff, 0.0))
        attn.append(qk[j][0:c, :] * (decay * masks_ref[M_CAUSAL]))
        a_mat.append(qk[j][c:2 * c, :] * (decay * masks_ref[M_STRICT]))
    t_inv = _tri_inv(a_mat, c)
    w_u = [_bdot(t_inv[j], jnp.concatenate([y[i][:, lanes(2 * GDN_QK, h)] * col(pre[i]["beta"], h),
                                            kn[j] * col(pre[i]["beta_e"], h)], axis=1))
           for j, (i, h) in enumerate(probs)]
    x_s = [_bdot(jnp.concatenate([w_u[j][:, GDN_DV:2 * GDN_DV], qn[j] * col(pre[i]["e_gc"], h)], axis=0),
                 s_ref[i, h]) for j, (i, h) in enumerate(probs)]
    u = [w_u[j][:, 0:GDN_DV] - x_s[j][0:c, :] for j in range(len(probs))]
    o = [x_s[j][c:2 * c, :] + _bdot(attn[j], u[j]) for j in range(len(probs))]
    ds = [_bdot_tn(kn[j] * col(pre[i]["e_kd"], h), u[j]) for j, (i, h) in enumerate(probs)]
    gnw = gnw_ref[...]
    for j, (i, h) in enumerate(probs):
        s_ref[i, h] = s_ref[i, h] * col(pre[i]["e_last"], h) + ds[j]
        out = o[j] * _rms_scale(o[j]) * gnw
        z = p_buf[pslot, seq_rows(i), GDN_CONV_DIM + h * GDN_DV:GDN_CONV_DIM + (h + 1) * GDN_DV]
        out = out * _silu(z.astype(F32))
        o_ref[i, :, lanes(0, h)] = out.astype(BF16)
    tail_ref[...] = xbuf_ref[:, 0:CARRY_ROWS, :]


def _gdn_mixer(x3, norm_w, w_gdn, w_ba, conv_w, alog_row, dtb_row, gnw_row, conv0, s0):
    batch, seq, d = x3.shape
    c = min(CHUNK, seq)
    bb = min(GDN_ROWS_PER_STEP // c, batch)
    n = seq // c
    masks, shift = _gdn_constants(c)
    const = lambda shape: pl.BlockSpec(shape, lambda b, t: (0,) * len(shape))
    state_spec = pl.BlockSpec((bb, GDN_HEADS, GDN_DK, GDN_DV), lambda b, t: (b, 0, 0, 0))
    carry_spec = pl.BlockSpec((bb, CARRY_ROWS, GDN_CONV_DIM), lambda b, t: (b, 0, 0))
    return pl.pallas_call(
        functools.partial(_gdn_kernel, c=c, bb=bb),
        grid=(batch // bb, n + 1),
        in_specs=[
            pl.BlockSpec((bb, c, d), lambda b, t: (b, jnp.minimum(t, n - 1), 0)),
            const((1, d)),
            const(w_gdn.shape),
            const(w_ba.shape),
            const((CONV_W, GDN_CONV_DIM)),
            const((1, LANES)),
            const((1, LANES)),
            const((1, GDN_DV)),
            carry_spec,
            state_spec,
            const(masks.shape),
            const(shift.shape),
        ],
        out_specs=[
            pl.BlockSpec((bb, c, GDN_V), lambda b, t: (b, jnp.maximum(t - 1, 0), 0)),
            state_spec,
            carry_spec,
        ],
        out_shape=[
            jax.ShapeDtypeStruct((batch, seq, GDN_V), BF16),
            jax.ShapeDtypeStruct((batch, GDN_HEADS, GDN_DK, GDN_DV), F32),
            jax.ShapeDtypeStruct((batch, CARRY_ROWS, GDN_CONV_DIM), BF16),
        ],
        scratch_shapes=[
            pltpu.VMEM((bb, CARRY_ROWS + c, GDN_CONV_DIM), BF16),
            pltpu.VMEM((2, bb * c, GDN_P_WIDTH), BF16),
            pltpu.VMEM((2, bb * c, BA_WIDTH), F32),
        ],
        compiler_params=_params(2, VMEM_LIMIT_BIG_TILE_BYTES),
        name="gdn_mixer",
    )(x3, norm_w, w_gdn, w_ba, conv_w, alog_row, dtb_row, gnw_row, conv0, s0, masks, shift)


def _rope_kernel(inv_ref, cos_ref, sin_ref, *, rows, offset):
    pos = lax.broadcasted_iota(jnp.int32, (rows, LANES), 0) + (pl.program_id(0) * rows + offset)
    ang = pos.astype(F32) * inv_ref[...]
    cos_ref[...] = jnp.cos(ang)
    sin_ref[...] = jnp.sin(ang)


def _rope_tables(inv_row, seq, offset):
    rows = min(512, seq)
    return pl.pallas_call(
        functools.partial(_rope_kernel, rows=rows, offset=offset),
        grid=(seq // rows,),
        in_specs=[pl.BlockSpec((1, LANES), lambda i: (0, 0))],
        out_specs=[pl.BlockSpec((rows, LANES), lambda i: (i, 0))] * 2,
        out_shape=[jax.ShapeDtypeStruct((seq, LANES), F32)] * 2,
        compiler_params=_params(1),
        name="rope_tables",
    )(inv_row)


RET_CHUNK = 256
RET_ROWS_PER_STEP = 512


def _pair_split_matrix():
    perm = np.concatenate([np.arange(0, RET_DK, 2), np.arange(1, RET_DK, 2)])
    return np.eye(RET_DK, dtype=np.float32)[:, perm]


RET_P_WIDTH = 2 * RET_QK + 2 * RET_V
RET_PROJ_TILE = 512


def _ret_kernel(x_ref, nw_ref, w_ref, cos_ref, sin_ref, gnw_ref, s0_ref, split_ref, split_t_ref,
                o_ref, s_ref, dmat_ref, scale_ref, p_buf, *, c, bb, n_blocks):
    t = pl.program_id(1)
    nslot = t % 2
    pslot = 1 - nslot
    probs = [(i, h) for i in range(bb) for h in range(RET_HEADS)]

    def load_state():
        for i, h in probs:
            s_ref[i, h] = _select_rows_exact(split_t_ref[...], s0_ref[i, h])

    @pl.when(t == 0)
    def _():
        p_buf[1] = jnp.zeros(p_buf.shape[1:], BF16)
        load_state()

    pl.when(t == 1)(load_state)

    @pl.when(jnp.logical_and(pl.program_id(0) == 0, t == 0))
    def _():
        r = lax.broadcasted_iota(jnp.int32, (c, c), 0)
        q_i = lax.broadcasted_iota(jnp.int32, (c, c), 1)
        causal = r >= q_i
        dpos = (r - q_i).astype(F32)
        tok = lax.broadcasted_iota(jnp.int32, (c, LANES), 0).astype(F32)
        for h in range(RET_HEADS):
            lg = math.log(1.0 - 2.0 ** (-5.0 - h))
            dmat = jnp.where(causal, jnp.exp(jnp.where(causal, dpos * lg, 0.0)), 0.0)
            dmat_ref[h] = dmat * (RET_DK ** -0.5)
            scale_ref[0, h] = jnp.exp((tok + 1.0) * lg)
            scale_ref[1, h] = jnp.exp((c - 1.0 - tok) * lg) * (RET_DK ** -0.5)

    x = x_ref[...].reshape(bb * c, x_ref.shape[-1])
    h_next = (x * _rms_scale(x) * nw_ref[...]).astype(BF16)
    proj_tiles = [slice(lo, lo + RET_PROJ_TILE) for lo in range(0, RET_P_WIDTH, RET_PROJ_TILE)]

    def project(tiles):
        for cols in tiles:
            p_buf[nslot, :, cols] = jnp.dot(h_next, w_ref[:, cols], preferred_element_type=F32).astype(BF16)

    cos = cos_ref[...]
    sin = sin_ref[...]
    half = RET_DK // 2

    def rotate(x):
        xe, xo = x[:, 0:half], x[:, half:RET_DK]
        return jnp.concatenate([xe * cos - xo * sin, xo * cos + xe * sin], axis=1)

    both = lambda arr: jnp.concatenate([arr, arr], axis=1)
    prev = lambda g, i, h: p_buf[pslot, i * c:(i + 1) * c, g * RET_QK + h * RET_DK:g * RET_QK + (h + 1) * RET_DK]
    lanes = lambda h: slice(h * RET_DK, (h + 1) * RET_DK)
    n_tiles = len(proj_tiles)
    project(proj_tiles[0:n_tiles // 2])
    qr = [rotate(prev(0, i, h).astype(F32)) for i, h in probs]
    kr = [rotate(prev(1, i, h).astype(F32)) for i, h in probs]
    attn = [_bdot_nt(qr[j], kr[j]) * dmat_ref[h] for j, (i, h) in enumerate(probs)]
    o = [_bdot(qr[j] * both(scale_ref[0, h]), s_ref[i, h]) + _bdot(attn[j], prev(2, i, h))
         for j, (i, h) in enumerate(probs)]
    ds = [_bdot_tn(kr[j] * both(scale_ref[1, h]), prev(2, i, h)) for j, (i, h) in enumerate(probs)]
    project(proj_tiles[n_tiles // 2:])
    for j, (i, h) in enumerate(probs):
        s_ref[i, h] = s_ref[i, h] * math.exp(c * math.log(1.0 - 2.0 ** (-5.0 - h))) + ds[j]
        mu = jnp.mean(o[j], axis=-1, keepdims=True)
        dev = o[j] - mu
        var = jnp.mean(jnp.square(dev), axis=-1, keepdims=True)
        out = dev * lax.rsqrt(var + GN_EPS) * gnw_ref[:, lanes(h)]
        out = out * _silu(prev(3, i, h).astype(F32))
        o_ref[i, :, lanes(h)] = out.astype(BF16)

    @pl.when(t == n_blocks)
    def _():
        for i, h in probs:
            s_ref[i, h] = _select_rows_exact(split_ref[...], s_ref[i, h])


def _ret_mixer(x3, norm_w, w_ret, cos, sin, gnw_row, s0):
    batch, seq, d = x3.shape
    c = min(RET_CHUNK, seq)
    bb = min(RET_ROWS_PER_STEP // c, batch)
    n = seq // c
    split = _pair_split_matrix()
    const = lambda shape: pl.BlockSpec(shape, lambda b, t: (0,) * len(shape))
    state_spec = pl.BlockSpec((bb, RET_HEADS, RET_DK, RET_DV), lambda b, t: (b, 0, 0, 0))
    prev_rows = lambda b, t: (jnp.maximum(t - 1, 0), 0)
    return pl.pallas_call(
        functools.partial(_ret_kernel, c=c, bb=bb, n_blocks=n),
        grid=(batch // bb, n + 1),
        in_specs=[
            pl.BlockSpec((bb, c, d), lambda b, t: (b, jnp.minimum(t, n - 1), 0)),
            const((1, d)),
            const(w_ret.shape),
            pl.BlockSpec((c, LANES), prev_rows),
            pl.BlockSpec((c, LANES), prev_rows),
            const((1, RET_V)),
            state_spec,
            const((RET_DK, RET_DK)),
            const((RET_DK, RET_DK)),
        ],
        out_specs=[
            pl.BlockSpec((bb, c, RET_V), lambda b, t: (b, jnp.maximum(t - 1, 0), 0)),
            state_spec,
        ],
        out_shape=[
            jax.ShapeDtypeStruct((batch, seq, RET_V), BF16),
            jax.ShapeDtypeStruct((batch, RET_HEADS, RET_DK, RET_DV), F32),
        ],
        scratch_shapes=[pltpu.VMEM((RET_HEADS, c, c), F32), pltpu.VMEM((2, RET_HEADS, c, LANES), F32),
                        pltpu.VMEM((2, bb * c, RET_P_WIDTH), BF16)],
        compiler_params=_params(2, VMEM_LIMIT_BIG_TILE_BYTES),
        name="ret_mixer",
    )(x3, norm_w, w_ret, cos, sin, gnw_row, s0, jnp.asarray(split, dtype=BF16), jnp.asarray(split.T, dtype=BF16))


def _out_proj_kernel(x_ref, ma_ref, mr_ref, woa_ref, wor_ref, nw_ref, x1_ref, h2_ref):
    x1 = x_ref[...] + jnp.dot(ma_ref[...], woa_ref[...], preferred_element_type=F32)
    x1 = x1 + jnp.dot(mr_ref[...], wor_ref[...], preferred_element_type=F32)
    x1_ref[...] = x1
    h2_ref[...] = (x1 * _rms_scale(x1) * nw_ref[...]).astype(BF16)


def _out_proj(x2d, mix_a, mix_r, w_out_a, w_out_r, norm_w):
    t, d = x2d.shape
    tm = min(OUT_PROJ_TM, t)
    return pl.pallas_call(
        _out_proj_kernel,
        grid=(t // tm,),
        in_specs=[
            pl.BlockSpec((tm, d), lambda i: (i, 0)),
            pl.BlockSpec((tm, GDN_V), lambda i: (i, 0)),
            pl.BlockSpec((tm, RET_V), lambda i: (i, 0)),
            pl.BlockSpec((GDN_V, d), lambda i: (0, 0)),
            pl.BlockSpec((RET_V, d), lambda i: (0, 0)),
            pl.BlockSpec((1, d), lambda i: (0, 0)),
        ],
        out_specs=[
            pl.BlockSpec((tm, d), lambda i: (i, 0)),
            pl.BlockSpec((tm, d), lambda i: (i, 0)),
        ],
        out_shape=[
            jax.ShapeDtypeStruct((t, d), F32),
            jax.ShapeDtypeStruct((t, d), BF16),
        ],
        compiler_params=_params(1),
        name="out_proj",
    )(x2d, mix_a, mix_r, w_out_a, w_out_r, norm_w)


def _ffn_kernel(h2_ref, x1_ref, wg_ref, wu_ref, wd_ref, fw_ref, y_ref, *, n_ff, final_norm):
    f = pl.program_id(1)

    @pl.when(f == 0)
    def _():
        y_ref[...] = x1_ref[...]

    h = h2_ref[...]
    gate = jnp.dot(h, wg_ref[...], preferred_element_type=F32)
    up = jnp.dot(h, wu_ref[...], preferred_element_type=F32)
    act = (_silu(gate) * up).astype(BF16)
    y_ref[...] += jnp.dot(act, wd_ref[...], preferred_element_type=F32)

    if final_norm:
        @pl.when(f == n_ff - 1)
        def _():
            x2 = y_ref[...]
            y_ref[...] = x2 * _rms_scale(x2) * fw_ref[...]


def _ffn(h2, x1, w_gate_up, w_down, final_w, final_norm):
    t, d = x1.shape
    d_ff = w_down.shape[0]
    tm = min(FFN_TM, t)
    tf = FFN_TF if tm == FFN_TM else FFN_TF_FEW_ROWS
    n_ff = d_ff // tf
    return pl.pallas_call(
        functools.partial(_ffn_kernel, n_ff=n_ff, final_norm=final_norm),
        grid=(t // tm, n_ff),
        in_specs=[
            pl.BlockSpec((tm, d), lambda i, f: (i, 0)),
            pl.BlockSpec((tm, d), lambda i, f: (i, 0)),
            pl.BlockSpec((d, tf), lambda i, f: (0, f)),
            pl.BlockSpec((d, tf), lambda i, f: (0, n_ff + f)),
            pl.BlockSpec((tf, d), lambda i, f: (f, 0)),
            pl.BlockSpec((1, d), lambda i, f: (0, 0)),
        ],
        out_specs=pl.BlockSpec((tm, d), lambda i, f: (i, 0)),
        out_shape=jax.ShapeDtypeStruct((t, d), F32),
        compiler_params=_params(2, VMEM_LIMIT_BIG_TILE_BYTES),
        name="ffn",
    )(h2, x1, w_gate_up, w_gate_up, w_down, final_w)


W_IN_PREP_ROWS = 256


def _w_in_layout_kernel(w_ref, split_ref, gdn_ref, ret_ref, ba_ref):
    o_b = GDN_CONV_DIM + GDN_V
    o_qr = o_b + 2 * GDN_HEADS
    gdn_ref[...] = w_ref[:, 0:o_b]
    for head in range(2 * RET_HEADS):
        src = w_ref[:, o_qr + head * RET_DK:o_qr + (head + 1) * RET_DK]
        ret_ref[:, head * RET_DK:(head + 1) * RET_DK] = jnp.dot(
            src, split_ref[...], preferred_element_type=F32).astype(BF16)
    ret_ref[:, 2 * RET_QK:RET_P_WIDTH] = w_ref[:, o_qr + 2 * RET_QK:o_qr + RET_P_WIDTH]
    gates = w_ref[:, o_b:o_qr]
    pad = jnp.zeros((gates.shape[0], LANES - GDN_HEADS), BF16)
    ba_ref[...] = jnp.concatenate([gates[:, 0:GDN_HEADS], pad, gates[:, GDN_HEADS:2 * GDN_HEADS], pad], axis=1)


def _w_in_layout(w_in_bf16):
    d, width = w_in_bf16.shape
    rows = min(W_IN_PREP_ROWS, d)
    out = lambda n: pl.BlockSpec((rows, n), lambda i: (i, 0))
    return pl.pallas_call(
        _w_in_layout_kernel,
        grid=(d // rows,),
        in_specs=[pl.BlockSpec((rows, width), lambda i: (i, 0)),
                  pl.BlockSpec((RET_DK, RET_DK), lambda i: (0, 0))],
        out_specs=[out(GDN_P_WIDTH), out(RET_P_WIDTH), out(BA_WIDTH)],
        out_shape=[jax.ShapeDtypeStruct((d, n), BF16) for n in (GDN_P_WIDTH, RET_P_WIDTH, BA_WIDTH)],
        compiler_params=_params(1),
        name="w_in_layout",
    )(w_in_bf16, jnp.asarray(_pair_split_matrix(), dtype=BF16))


def _prep_layer_weights(w_in, w_out, w_gate_up, w_down, a_log, dt_bias, gdn_norm_w, ret_gn_w,
                        attn_norm_w, ffn_norm_w, conv_w):
    d = w_in.shape[0]
    w_gdn, w_ret, w_ba = _w_in_layout(w_in.astype(BF16))
    row_pad = lambda v: jnp.pad(v.astype(F32), (0, LANES - v.shape[0])).reshape(1, LANES)
    return dict(
        w_gdn=w_gdn, w_ret=w_ret, w_ba=w_ba,
        w_out_a=w_out[0:GDN_V].astype(BF16), w_out_r=w_out[GDN_V:].astype(BF16),
        w_gate_up=w_gate_up.astype(BF16), w_down=w_down.astype(BF16),
        alog_row=row_pad(a_log), dtb_row=row_pad(dt_bias),
        gnw_row=gdn_norm_w.reshape(1, GDN_DV).astype(F32),
        ret_gnw_row=ret_gn_w.reshape(1, RET_V).astype(F32),
        attn_norm_w=attn_norm_w.reshape(1, d), ffn_norm_w=ffn_norm_w.reshape(1, d),
        conv_w=conv_w,
    )


def _layer(x2d, batch, seq, cos, sin, conv_state, s_gdn, s_ret, lw, final_w, final_norm):
    d = x2d.shape[1]
    conv0 = jnp.pad(conv_state.astype(BF16), ((0, 0), (CARRY_ROWS - (CONV_W - 1), 0), (0, 0)))
    mix_a, s_gdn_new, tail = _gdn_mixer(x2d.reshape(batch, seq, d), lw["attn_norm_w"], lw["w_gdn"], lw["w_ba"],
                                        lw["conv_w"], lw["alog_row"], lw["dtb_row"], lw["gnw_row"],
                                        conv0, s_gdn.astype(F32))
    mix_r, s_ret_new = _ret_mixer(x2d.reshape(batch, seq, d), lw["attn_norm_w"], lw["w_ret"], cos, sin,
                                  lw["ret_gnw_row"], s_ret.astype(F32))
    mix_a = mix_a.reshape(batch * seq, GDN_V)
    mix_r = mix_r.reshape(batch * seq, RET_V)
    x1, h2 = _out_proj(x2d, mix_a, mix_r, lw["w_out_a"], lw["w_out_r"], lw["ffn_norm_w"])
    y = _ffn(h2, x1, lw["w_gate_up"], lw["w_down"], final_w, final_norm)
    rows = tail[:, CARRY_ROWS - (CONV_W - 1):, :].astype(conv_state.dtype)
    return y, rows, s_gdn_new.astype(s_gdn.dtype), s_ret_new.astype(s_ret.dtype)


def kernel(x_prompt, x_sample, state_gdn_conv, state_gdn, state_ret, attn_norm_w, w_in, conv_w, a_log, dt_bias, gdn_norm_w, ret_gn_w, w_out, ffn_norm_w, w_gate_up, w_down, final_norm_w):
    depth = w_in.shape[0]
    bp, lp, d = x_prompt.shape
    bs, ls, _ = x_sample.shape
    assert lp >= CONV_W - 1 and ls >= CONV_W - 1
    inv = 1.0 / (ROPE_BASE ** jnp.linspace(0.0, 1.0, RET_DK // 2, dtype=F32))
    inv_row = inv.reshape(1, LANES)
    cos_p, sin_p = _rope_tables(inv_row, lp, 0)
    cos_s, sin_s = _rope_tables(inv_row, ls, PAST_LEN)
    final_w = final_norm_w.reshape(1, d)
    hp = x_prompt.reshape(bp * lp, d)
    hs = x_sample.reshape(bs * ls, d)
    outs = [[] for _ in range(6)]
    for l in range(depth):
        lw = _prep_layer_weights(w_in[l], w_out[l], w_gate_up[l], w_down[l], a_log[l], dt_bias[l],
                                 gdn_norm_w[l], ret_gn_w[l], attn_norm_w[l], ffn_norm_w[l], conv_w[l])
        last = l == depth - 1
        conv0 = jnp.zeros((bp, CONV_W - 1, GDN_CONV_DIM), x_prompt.dtype)
        sg0 = jnp.zeros((bp, GDN_HEADS, GDN_DK, GDN_DV), state_gdn.dtype)
        sr0 = jnp.zeros((bp, RET_HEADS, RET_DK, RET_DV), state_ret.dtype)
        hp, c_p, g_p, r_p = _layer(hp, bp, lp, cos_p, sin_p, conv0, sg0, sr0, lw, final_w, last)
        hs, c_s, g_s, r_s = _layer(hs, bs, ls, cos_s, sin_s, state_gdn_conv[l], state_gdn[l],
                                   state_ret[l], lw, final_w, last)
        for acc, val in zip(outs, (c_p, g_p, r_p, c_s, g_s, r_s)):
            acc.append(val)
    return (hp.reshape(bp, lp, d), hs.reshape(bs, ls, d)) + tuple(jnp.stack(o) for o in outs)
```
